```python
import math
import jax, jax.numpy as jnp
from jax import lax
import numpy as np

D_MODEL = 1024
BATCH = 4
SEQ = 4096
DEPTH = 4
DEC_BATCH = 128
DEC_SEQ = 8
PAST_LEN = 2048
PAGE_SIZE = 128

N_MIXERS = 2
N_POOL_LAYERS = (DEPTH + 1) // 2
N_ATTN_LAYERS = DEPTH // 2
N_HEADS = 8
HEAD_DIM = D_MODEL // (2 * N_HEADS)
VAL_DIM = 2 * HEAD_DIM
HALF = HEAD_DIM // 2
ROPE_THETA = 10000.0
Q_BLOCK = 128
POOL_WINDOWS = (2, 4, 8, 16)
N_POOL_GROUPS = 4
GROUP_CH = D_MODEL // N_POOL_GROUPS
MAX_WIN = 16
POOL_BUF = MAX_WIN - 1
N_EXPERTS = 16
N_EXPERT_GROUPS = 4
EXPERTS_PER_GROUP = N_EXPERTS // N_EXPERT_GROUPS
TOP_K = 2
D_EXPERT = D_MODEL // 4
EPS = 1e-6

kernel_name = 'hybrid_pool_diffattn_grouped_moe_step'


def rmsnorm(x, g):
    xf = x.astype(jnp.float32)
    y = xf * lax.rsqrt(jnp.mean(xf * xf, axis=-1, keepdims=True) + EPS)
    return (y * g.astype(jnp.float32)).astype(x.dtype)


def ada_mods(c, w, b):
    m = jax.nn.silu(c) @ w + b
    return tuple(t[:, None, :] for t in jnp.split(m, 6, axis=-1))


def rope(x, pos):
    inv = 1.0 / (ROPE_THETA ** (jnp.arange(0, HEAD_DIM, 2, dtype=jnp.float32) / HEAD_DIM))
    ang = pos.astype(jnp.float32)[:, None] * inv[None, :]
    cos = jnp.cos(ang)[:, None, None, :]
    sin = jnp.sin(ang)[:, None, None, :]
    xf = x.astype(jnp.float32)
    x1, x2 = xf[..., :HALF], xf[..., HALF:]
    return jnp.concatenate([x1 * cos - x2 * sin, x2 * cos + x1 * sin], axis=-1).astype(x.dtype)


def pool_mixer(h_ext, n_new, pos0, w_grp, scale):
    b, L, _ = h_ext.shape
    hf = h_ext.astype(jnp.float32)
    csum = jnp.cumsum(jnp.pad(hf, ((0, 0), (MAX_WIN, 0), (0, 0))), axis=1)
    end = csum[:, MAX_WIN + L - n_new:]
    pos = pos0 + jnp.arange(n_new)
    means = []
    for g, w in enumerate(POOL_WINDOWS):
        sl = slice(g * GROUP_CH, (g + 1) * GROUP_CH)
        start = csum[:, MAX_WIN + L - n_new - w:MAX_WIN + L - w, sl]
        cnt = jnp.minimum(w, pos + 1).astype(jnp.float32)[None, :, None]
        means.append((end[..., sl] - start) / cnt)
    pooled = jnp.concatenate(means, axis=-1) - hf[:, L - n_new:]
    pooled = pooled.reshape(b, n_new, N_POOL_GROUPS, GROUP_CH).astype(h_ext.dtype)
    out = jnp.einsum('btgc,gce->btge', pooled, w_grp).reshape(b, n_new, D_MODEL)
    return out * scale


def diff_lambda(lq1, lk1, lq2, lk2, lam_init):
    f = jnp.float32
    return (jnp.exp(jnp.sum(lq1.astype(f) * lk1.astype(f)))
            - jnp.exp(jnp.sum(lq2.astype(f) * lk2.astype(f))) + lam_init)


def attn_qkv(h, w_qkv, pos):
    b, n, _ = h.shape
    q, k, v = jnp.split(h @ w_qkv, 3, axis=-1)
    q = rope(q.reshape(b, n, N_HEADS, 2, HEAD_DIM), pos)
    k = rope(k.reshape(b, n, N_HEADS, 2, HEAD_DIM), pos)
    v = v.reshape(b, n, N_HEADS, VAL_DIM)
    return q, k, v


def diff_attend(q, k, v, q_pos, k_pos, lam):
    s = jnp.einsum('bqhcd,bkhcd->bchqk', q, k, preferred_element_type=jnp.float32) * (HEAD_DIM ** -0.5)
    mask = k_pos[None, :] <= q_pos[:, None]
    s = jnp.where(mask, s, -jnp.inf)
    p = jax.nn.softmax(s, axis=-1)
    a = p[:, 0] - lam * p[:, 1]
    o = jnp.einsum('bhqk,bkhe->bqhe', a.astype(v.dtype), v, preferred_element_type=jnp.float32)
    return o.astype(v.dtype)


def prompt_diff_attention(q, k, v, lam):
    b, n = q.shape[:2]
    nb = n // Q_BLOCK
    qb = q.reshape(b, nb, Q_BLOCK, N_HEADS, 2, HEAD_DIM).swapaxes(0, 1)
    starts = jnp.arange(nb) * Q_BLOCK
    k_pos = jnp.arange(n)

    def one_block(args):
        q_blk, s0 = args
        return diff_attend(q_blk, k, v, s0 + jnp.arange(Q_BLOCK), k_pos, lam)

    o = lax.map(one_block, (qb, starts))
    return o.swapaxes(0, 1).reshape(b, n, N_HEADS, VAL_DIM)


def attn_out(o, subln_g, lam_init, w_o):
    b, n = o.shape[:2]
    o = rmsnorm(o, subln_g) * (1.0 - lam_init)
    return o.reshape(b, n, D_MODEL) @ w_o


def moe(h, router_w, router_bias, wg, wu, wd):
    b, t, _ = h.shape
    s = jax.nn.sigmoid(jnp.einsum('btd,de->bte', h, router_w, preferred_element_type=jnp.float32))
    sb = s + router_bias.astype(jnp.float32)
    grp = sb.reshape(b, t, N_EXPERT_GROUPS, EXPERTS_PER_GROUP)
    gscore = lax.top_k(grp, TOP_K)[0].sum(-1)
    gsel = jnp.argmax(gscore, axis=-1)
    in_grp = jnp.repeat(gsel[..., None] == jnp.arange(N_EXPERT_GROUPS), EXPERTS_PER_GROUP, axis=-1)
    _, idx = lax.top_k(jnp.where(in_grp, sb, -jnp.inf), TOP_K)
    w = jnp.take_along_axis(s, idx, axis=-1)
    w = w / jnp.sum(w, axis=-1, keepdims=True)
    gates = jnp.sum(jax.nn.one_hot(idx, N_EXPERTS, dtype=jnp.float32) * w[..., None], axis=-2)
    hg = jnp.einsum('btd,edf->btef', h, wg)
    hu = jnp.einsum('btd,edf->btef', h, wu)
    act = jax.nn.silu(hg) * hu * gates[..., None].astype(h.dtype)
    return jnp.einsum('btef,efd->btd', act, wd)


def setup_inputs(seed: int = 0) -> dict:
    key = jax.random.key(seed)
    ks = jax.random.split(key, 32)
    f32 = jnp.float32
    n_pages = PAST_LEN // PAGE_SIZE
    n_pool_pages = (DEC_BATCH * n_pages * 5) // 4

    def nrm(k, shape, s):
        return jax.random.normal(k, shape, f32) * s

    page_table = jax.random.permutation(ks[7], n_pool_pages)[:DEC_BATCH * n_pages]
    page_table = page_table.reshape(DEC_BATCH, n_pages).astype(jnp.int32)
    cache_shape = (N_ATTN_LAYERS, n_pool_pages, PAGE_SIZE, N_HEADS, VAL_DIM)
    return {
        'x_prompt': nrm(ks[0], (BATCH, SEQ, D_MODEL), 1.0),
        'x_sample': nrm(ks[1], (DEC_BATCH, DEC_SEQ, D_MODEL), 1.0),
        'c_prompt': nrm(ks[2], (BATCH, D_MODEL), 1.0),
        'c_sample': nrm(ks[3], (DEC_BATCH, D_MODEL), 1.0),
        'state_pool': nrm(ks[4], (N_POOL_LAYERS, DEC_BATCH, POOL_BUF, D_MODEL), 1.0),
        'cache_k': nrm(ks[5], cache_shape, 1.0),
        'cache_v': nrm(ks[6], cache_shape, 1.0),
        'page_table': page_table,
        'ada_w': nrm(ks[8], (DEPTH, D_MODEL, 6 * D_MODEL), 0.5 * D_MODEL ** -0.5),
        'ada_b': nrm(ks[9], (DEPTH, 6 * D_MODEL), 0.01),
        'norm_mix': 1.0 + nrm(ks[10], (DEPTH, D_MODEL), 0.1),
        'norm_ffn': 1.0 + nrm(ks[11], (DEPTH, D_MODEL), 0.1),
        'norm_final': 1.0 + nrm(ks[12], (D_MODEL,), 0.1),
        'pool_w': nrm(ks[13], (N_POOL_LAYERS, N_POOL_GROUPS, GROUP_CH, GROUP_CH), GROUP_CH ** -0.5),
        'pool_scale': 1.0 + nrm(ks[14], (N_POOL_LAYERS, D_MODEL), 0.1),
        'attn_w_qkv': nrm(ks[15], (N_ATTN_LAYERS, D_MODEL, 3 * D_MODEL), D_MODEL ** -0.5),
        'attn_w_o': nrm(ks[16], (N_ATTN_LAYERS, D_MODEL, D_MODEL), D_MODEL ** -0.5),
        'lambda_q1': nrm(ks[17], (N_ATTN_LAYERS, HEAD_DIM), 0.1),
        'lambda_k1': nrm(ks[18], (N_ATTN_LAYERS, HEAD_DIM), 0.1),
        'lambda_q2': nrm(ks[19], (N_ATTN_LAYERS, HEAD_DIM), 0.1),
        'lambda_k2': nrm(ks[20], (N_ATTN_LAYERS, HEAD_DIM), 0.1),
        'subln': 1.0 + nrm(ks[21], (N_ATTN_LAYERS, VAL_DIM), 0.1),
        'router_w': nrm(ks[22], (D_MODEL, N_EXPERTS), D_MODEL ** -0.5),
        'router_bias': nrm(ks[23], (N_EXPERTS,), 0.01),
        'moe_w_gate': nrm(ks[24], (DEPTH, N_EXPERTS, D_MODEL, D_EXPERT), D_MODEL ** -0.5),
        'moe_w_up': nrm(ks[25], (DEPTH, N_EXPERTS, D_MODEL, D_EXPERT), D_MODEL ** -0.5),
        'moe_w_down': nrm(ks[26], (DEPTH, N_EXPERTS, D_EXPERT, D_MODEL), D_EXPERT ** -0.5),
    }


def reference(x_prompt, x_sample, c_prompt, c_sample, state_pool, cache_k, cache_v, page_table,
              ada_w, ada_b, norm_mix, norm_ffn, norm_final, pool_w, pool_scale,
              attn_w_qkv, attn_w_o, lambda_q1, lambda_k1, lambda_q2, lambda_k2, subln,
              router_w, router_bias, moe_w_gate, moe_w_up, moe_w_down):
    xp, xs = x_prompt, x_sample
    bp, n_p = xp.shape[:2]
    bs, n_s = xs.shape[:2]
    pos_p = jnp.arange(n_p)
    pos_s = PAST_LEN + jnp.arange(n_s)
    pool_p, pool_s, kp_rows, vp_rows, ks_rows, vs_rows = [], [], [], [], [], []
    for i in range(DEPTH):
        sh1p, sc1p, g1p, sh2p, sc2p, g2p = ada_mods(c_prompt, ada_w[i], ada_b[i])
        sh1s, sc1s, g1s, sh2s, sc2s, g2s = ada_mods(c_sample, ada_w[i], ada_b[i])
        hp = rmsnorm(xp, norm_mix[i]) * (1 + sc1p) + sh1p
        hs = rmsnorm(xs, norm_mix[i]) * (1 + sc1s) + sh1s
        j = i // N_MIXERS
        if i % N_MIXERS == 0:
            yp = pool_mixer(hp, n_p, 0, pool_w[j], pool_scale[j])
            hs_ext = jnp.concatenate([state_pool[j].astype(hs.dtype), hs], axis=1)
            ys = pool_mixer(hs_ext, n_s, PAST_LEN, pool_w[j], pool_scale[j])
            pool_p.append(hp[:, n_p - POOL_BUF:])
            pool_s.append(hs_ext[:, hs_ext.shape[1] - POOL_BUF:])
        else:
            lam_init = 0.8 - 0.6 * math.exp(-0.3 * i)
            lam = diff_lambda(lambda_q1[j], lambda_k1[j], lambda_q2[j], lambda_k2[j], lam_init)
            qp, kp, vp = attn_qkv(hp, attn_w_qkv[j], pos_p)
            yp = attn_out(prompt_diff_attention(qp, kp, vp, lam), subln[j], lam_init, attn_w_o[j])
            qs, kn, vn = attn_qkv(hs, attn_w_qkv[j], pos_s)
            k_past = cache_k[j, page_table].reshape(bs, -1, N_HEADS, VAL_DIM)
            v_past = cache_v[j, page_table].reshape(bs, -1, N_HEADS, VAL_DIM)
            kn_rows = kn.reshape(bs, n_s, N_HEADS, VAL_DIM)
            k_all = jnp.concatenate([k_past.astype(kn.dtype), kn_rows], axis=1)
            k_all = k_all.reshape(bs, -1, N_HEADS, 2, HEAD_DIM)
            v_all = jnp.concatenate([v_past.astype(vn.dtype), vn], axis=1)
            os_ = diff_attend(qs, k_all, v_all, pos_s, jnp.arange(PAST_LEN + n_s), lam)
            ys = attn_out(os_, subln[j], lam_init, attn_w_o[j])
            kp_rows.append(kp.reshape(bp, n_p, N_HEADS, VAL_DIM))
            vp_rows.append(vp)
            ks_rows.append(kn_rows)
            vs_rows.append(vn)
        xp = xp + g1p * yp
        xs = xs + g1s * ys
        hp = rmsnorm(xp, norm_ffn[i]) * (1 + sc2p) + sh2p
        hs = rmsnorm(xs, norm_ffn[i]) * (1 + sc2s) + sh2s
        xp = xp + g2p * moe(hp, router_w, router_bias, moe_w_gate[i], moe_w_up[i], moe_w_down[i])
        xs = xs + g2s * moe(hs, router_w, router_bias, moe_w_gate[i], moe_w_up[i], moe_w_down[i])
    y_prompt = rmsnorm(xp, norm_final)
    y_sample = rmsnorm(xs, norm_final)
    return (y_prompt, y_sample, jnp.stack(pool_p), jnp.stack(pool_s),
            jnp.stack(kp_rows), jnp.stack(vp_rows), jnp.stack(ks_rows), jnp.stack(vs_rows))
```

```python
import functools
import math

import jax
import jax.numpy as jnp
from jax import lax
from jax.experimental import pallas as pl
from jax.experimental.pallas import tpu as pltpu

F32 = jnp.float32
BF16 = jnp.bfloat16

D_MODEL = 1024
N_HEADS = 8
HEAD_DIM = 64
VAL_DIM = 128
ROPE_THETA = 10000.0
POOL_WINDOWS = (2, 4, 8, 16)
GROUP_CH = 256
MAX_WIN = 16
N_EXPERTS = 16
EXPERTS_PER_GROUP = 4
N_EXPERT_GROUPS = 4
D_EXPERT = 256
PAGE_SIZE = 128
EPS = 1e-6
N_MODS = 6

VMEM_LIMIT_BYTES = 52 * 1024 * 1024


def _cparams(*sem):
    return pltpu.CompilerParams(dimension_semantics=sem, vmem_limit_bytes=VMEM_LIMIT_BYTES)


def _normmod(x, g, sc, sh):
    r = lax.rsqrt(jnp.mean(x * x, axis=-1, keepdims=True) + EPS)
    return (x * r) * g * (1.0 + sc) + sh


def _dot(a, b):
    return jnp.dot(a, b, preferred_element_type=F32)


def _dot_nt(a, b):
    return lax.dot_general(a, b, (((1,), (1,)), ((), ())), preferred_element_type=F32)


def _ada_kernel(c_ref, w_ref, b_ref, os_ref, op_ref, *, n_s):
    c = c_ref[...]
    a = (c * jax.nn.sigmoid(c)).astype(BF16)
    r = _dot(a, w_ref[...].astype(BF16)) + b_ref[...]
    os_ref[...] = r[:n_s]
    op_ref[...] = r[n_s:]


def _ada_mods(c_sample, c_prompt, ada_w, ada_b, tn=768):
    depth, d, n = ada_w.shape
    n_s, n_p = c_sample.shape[0], c_prompt.shape[0]
    c_all = jnp.concatenate([c_sample, c_prompt], axis=0)
    return pl.pallas_call(
        functools.partial(_ada_kernel, n_s=n_s),
        grid=(depth, n // tn),
        in_specs=[
            pl.BlockSpec((n_s + n_p, d), lambda i, j: (0, 0)),
            pl.BlockSpec((None, d, tn), lambda i, j: (i, 0, j)),
            pl.BlockSpec((None, 1, tn), lambda i, j: (i, 0, j)),
        ],
        out_specs=[
            pl.BlockSpec((None, n_s, tn), lambda i, j: (i, 0, j)),
            pl.BlockSpec((None, n_p, tn), lambda i, j: (i, 0, j)),
        ],
        out_shape=[jax.ShapeDtypeStruct((depth, n_s, n), F32),
                   jax.ShapeDtypeStruct((depth, n_p, n), F32)],
        compiler_params=_cparams("arbitrary", "arbitrary"),
        name="ada_mods",
    )(c_all, ada_w, ada_b.reshape(depth, 1, n))


class _Group:
    def __init__(self, mods, per_token, seq):
        self.mods = mods
        self.per_token = per_token
        self.seq = seq

    def mod_spec(self, layer, chunk, tm):
        if self.per_token:
            return pl.BlockSpec((None, tm, D_MODEL), lambda t, *_: (layer, t, chunk))
        tpb = self.seq // tm
        return pl.BlockSpec((None, None, 1, D_MODEL), lambda t, *_: (layer, t // tpb, 0, chunk))


def _row_spec(layer, width=D_MODEL):
    return pl.BlockSpec((None, 1, width), lambda t, *_: (layer, 0, 0))


def _pool_p_kernel(x_ref, halo_ref, sh_ref, sc_ref, gt_ref, nrm_ref, w_ref, ps_ref,
                   xo_ref, st_ref, ext_ref, *, ts, tpb, rc):
    s = pl.program_id(0) % tpb
    g = nrm_ref[...]
    sc = sc_ref[...]
    sh = sh_ref[...]
    hh = _normmod(halo_ref[...], g, sc, sh)
    ext_ref[0:MAX_WIN, :] = jnp.where(s == 0, 0.0, hh)
    ext_ref[MAX_WIN:, :] = _normmod(x_ref[...], g, sc, sh)
    for r0 in range(0, ts, rc):
        pos = s * ts + r0 + lax.broadcasted_iota(jnp.int32, (rc, 1), 0)
        for gi, w in enumerate(POOL_WINDOWS):
            cs = slice(gi * GROUP_CH, (gi + 1) * GROUP_CH)
            base = MAX_WIN + r0
            h = ext_ref[base:base + rc, cs]
            acc = h
            for k in range(1, w):
                acc = acc + ext_ref[base - k:base - k + rc, cs]
            cnt = jnp.minimum(w, pos + 1).astype(F32)
            pooled = acc / cnt - h
            o = _dot(pooled.astype(BF16), w_ref[gi]) * ps_ref[:, cs]
            xo_ref[r0:r0 + rc, cs] = x_ref[r0:r0 + rc, cs] + gt_ref[:, cs] * o

    @pl.when(s == tpb - 1)
    def _():
        st_ref[...] = ext_ref[ts:ts + MAX_WIN, :]


def _pool_prompt(x, grp, layer, norm_mix, pool_w_bf, pool_scale, j, n_batch, ts=512, rc=128):
    t_tot, d = x.shape
    tpb = grp.seq // ts
    hb = ts // MAX_WIN
    return pl.pallas_call(
        functools.partial(_pool_p_kernel, ts=ts, tpb=tpb, rc=rc),
        grid=(t_tot // ts,),
        in_specs=[
            pl.BlockSpec((ts, d), lambda t: (t, 0)),
            pl.BlockSpec((MAX_WIN, d), lambda t: (jnp.maximum(t * hb - 1, 0), 0)),
            grp.mod_spec(layer, 0, ts), grp.mod_spec(layer, 1, ts), grp.mod_spec(layer, 2, ts),
            _row_spec(layer),
            pl.BlockSpec((None, 4, GROUP_CH, GROUP_CH), lambda t: (j, 0, 0, 0)),
            _row_spec(j),
        ],
        out_specs=[
            pl.BlockSpec((ts, d), lambda t: (t, 0)),
            pl.BlockSpec((None, MAX_WIN, d), lambda t: (t // tpb, 0, 0)),
        ],
        out_shape=[jax.ShapeDtypeStruct((t_tot, d), F32),
                   jax.ShapeDtypeStruct((n_batch, MAX_WIN, d), F32)],
        scratch_shapes=[pltpu.VMEM((ts + MAX_WIN, d), F32)],
        compiler_params=_cparams("arbitrary"),
        name="pool_prompt",
    )(x, x, grp.mods, grp.mods, grp.mods, norm_mix, pool_w_bf, pool_scale)


def _pool_s_kernel(x_ref, st_ref, sh_ref, sc_ref, gt_ref, nrm_ref, w_ref, ps_ref,
                   xo_ref, so_ref, ext_ref, *, bb, n_new):
    x = x_ref[...]
    h2 = _normmod(x, nrm_ref[...], sc_ref[...], sh_ref[...])
    ext_ref[:, 0:MAX_WIN, :] = st_ref[...]
    ext_ref[:, MAX_WIN:, :] = h2.reshape(bb, n_new, D_MODEL)
    outs = []
    for gi, w in enumerate(POOL_WINDOWS):
        cs = slice(gi * GROUP_CH, (gi + 1) * GROUP_CH)
        h = ext_ref[:, MAX_WIN:MAX_WIN + n_new, cs]
        acc = h
        for k in range(1, w):
            acc = acc + ext_ref[:, MAX_WIN - k:MAX_WIN - k + n_new, cs]
        pooled = (acc / float(w) - h).reshape(bb * n_new, GROUP_CH)
        outs.append(_dot(pooled.astype(BF16), w_ref[gi]))
    out = jnp.concatenate(outs, axis=-1) * ps_ref[...]
    xo_ref[...] = x + gt_ref[...] * out
    so_ref[...] = ext_ref[:, n_new:n_new + MAX_WIN, :]


def _pool_sample(x, state_pad, grp, layer, norm_mix, pool_w_bf, pool_scale, j, n_new, bb=32):
    t_tot, d = x.shape
    n_seq = t_tot // n_new
    tm = bb * n_new
    return pl.pallas_call(
        functools.partial(_pool_s_kernel, bb=bb, n_new=n_new),
        grid=(n_seq // bb,),
        in_specs=[
            pl.BlockSpec((tm, d), lambda t: (t, 0)),
            pl.BlockSpec((bb, MAX_WIN, d), lambda t: (t, 0, 0)),
            grp.mod_spec(layer, 0, tm), grp.mod_spec(layer, 1, tm), grp.mod_spec(layer, 2, tm),
            _row_spec(layer),
            pl.BlockSpec((None, 4, GROUP_CH, GROUP_CH), lambda t: (j, 0, 0, 0)),
            _row_spec(j),
        ],
        out_specs=[
            pl.BlockSpec((tm, d), lambda t: (t, 0)),
            pl.BlockSpec((bb, MAX_WIN, d), lambda t: (t, 0, 0)),
        ],
        out_shape=[jax.ShapeDtypeStruct((t_tot, d), F32),
                   jax.ShapeDtypeStruct((n_seq, MAX_WIN, d), F32)],
        scratch_shapes=[pltpu.VMEM((bb, MAX_WIN + n_new, d), F32)],
        compiler_params=_cparams("arbitrary"),
        name="pool_sample",
    )(x, state_pad, grp.mods, grp.mods, grp.mods, norm_mix, pool_w_bf, pool_scale)


def _qkv_kernel(x_ref, sh_ref, sc_ref, nrm_ref, w_ref, cos_ref, sin_ref,
                q_ref, kb_ref, vb_ref, kf_ref, vf_ref):
    h = _normmod(x_ref[...], nrm_ref[...], sc_ref[...], sh_ref[...]).astype(BF16)
    cos = cos_ref[...]
    sin = sin_ref[...]
    lane = lax.broadcasted_iota(jnp.int32, (1, VAL_DIM), 1)
    first_half = (lane % HEAD_DIM) < (HEAD_DIM // 2)

    def rope(xh):
        rot = jnp.where(first_half, pltpu.roll(xh, VAL_DIM - HEAD_DIM // 2, 1),
                        pltpu.roll(xh, HEAD_DIM // 2, 1))
        return xh * cos + rot * sin

    pw = 2 * VAL_DIM
    for c0 in range(0, D_MODEL, pw):
        q2 = _dot(h, w_ref[:, c0:c0 + pw])
        k2 = _dot(h, w_ref[:, D_MODEL + c0:D_MODEL + c0 + pw])
        v2 = _dot(h, w_ref[:, 2 * D_MODEL + c0:2 * D_MODEL + c0 + pw])
        for u in range(2):
            cs = slice(c0 + u * VAL_DIM, c0 + (u + 1) * VAL_DIM)
            us = slice(u * VAL_DIM, (u + 1) * VAL_DIM)
            q_ref[:, cs] = (rope(q2[:, us]) * (HEAD_DIM ** -0.5)).astype(BF16)
            kr = rope(k2[:, us])
            kf_ref[:, cs] = kr
            kb_ref[:, cs] = kr.astype(BF16)
        vf_ref[:, c0:c0 + pw] = v2
        vb_ref[:, c0:c0 + pw] = v2.astype(BF16)


def _qkv(x, grp, layer, norm_mix, wqkv_bf, cos_tab, sin_tab, tm):
    t_tot, d = x.shape
    n_pos_tiles = cos_tab.shape[0] // tm
    tok = pl.BlockSpec((tm, d), lambda t: (t, 0))
    tab = pl.BlockSpec((tm, VAL_DIM), lambda t: (t % n_pos_tiles, 0))
    return pl.pallas_call(
        _qkv_kernel,
        grid=(t_tot // tm,),
        in_specs=[tok, grp.mod_spec(layer, 0, tm), grp.mod_spec(layer, 1, tm), _row_spec(layer),
                  pl.BlockSpec((d, 3 * d), lambda t: (0, 0)), tab, tab],
        out_specs=[tok] * 5,
        out_shape=[jax.ShapeDtypeStruct((t_tot, d), BF16)] * 3 + [jax.ShapeDtypeStruct((t_tot, d), F32)] * 2,
        compiler_params=_cparams("arbitrary"),
        name="qkv_rope",
    )(x, grp.mods, grp.mods, norm_mix, wqkv_bf, cos_tab, sin_tab)


def _rope_tables(pos):
    inv = 1.0 / (ROPE_THETA ** (jnp.arange(0, HEAD_DIM, 2, dtype=F32) / HEAD_DIM))
    ang = pos.astype(F32)[:, None] * inv[None, :]
    cos, sin = jnp.cos(ang), jnp.sin(ang)
    cos_t = jnp.concatenate([cos, cos, cos, cos], axis=-1)
    sin_t = jnp.concatenate([-sin, sin, -sin, sin], axis=-1)
    return cos_t, sin_t


def _split_components(q):
    lane = lax.broadcasted_iota(jnp.int32, (1, VAL_DIM), 1)
    zero = jnp.zeros_like(q)
    return jnp.concatenate([jnp.where(lane < HEAD_DIM, q, zero), jnp.where(lane >= HEAD_DIM, q, zero)], axis=0)


def _diff_finish(acc, l, lam, subln, out_scale, r):
    o = acc / l
    od = o[:r] - lam * o[r:]
    y = od * lax.rsqrt(jnp.mean(od * od, axis=-1, keepdims=True) + EPS)
    return y * subln * out_scale


def _attn_p_kernel(lam_ref, q_ref, k_ref, v_ref, sub_ref, o_ref, m_ref, l_ref, acc_ref, *, tq, out_scale):
    qi = pl.program_id(2)
    qs = _split_components(q_ref[...])
    m_ref[...] = jnp.full(m_ref.shape, -jnp.inf, F32)
    l_ref[...] = jnp.zeros(l_ref.shape, F32)
    acc_ref[...] = jnp.zeros(acc_ref.shape, F32)

    def step(kj, masked):
        off = pl.multiple_of(kj * tq, tq)
        k = k_ref[pl.ds(off, tq), :]
        v = v_ref[pl.ds(off, tq), :]
        s = _dot_nt(qs, k)
        if masked:
            row = lax.broadcasted_iota(jnp.int32, (2 * tq, tq), 0) % tq
            col = lax.broadcasted_iota(jnp.int32, (2 * tq, tq), 1)
            s = jnp.where(col <= row, s, -jnp.inf)
        m_old = m_ref[...]
        m_new = jnp.maximum(m_old, jnp.max(s, axis=-1, keepdims=True))
        p = jnp.exp(s - m_new)
        alpha = jnp.exp(m_old - m_new)
        l_ref[...] = alpha * l_ref[...] + jnp.sum(p, axis=-1, keepdims=True)
        acc_ref[...] = alpha * acc_ref[...] + _dot(p.astype(BF16), v)
        m_ref[...] = m_new

    def body(kj, carry):
        step(kj, False)
        return carry

    lax.fori_loop(0, qi, body, 0)
    step(qi, True)
    y = _diff_finish(acc_ref[...], l_ref[...], lam_ref[0], sub_ref[...], out_scale, tq)
    o_ref[...] = y.astype(BF16)


def _attn_prompt(q, k, v, lam, subln_row, n_batch, seq, out_scale, tq=256):
    t_tot, d = q.shape
    nq = seq // tq
    return pl.pallas_call(
        functools.partial(_attn_p_kernel, tq=tq, out_scale=out_scale),
        grid=(n_batch, N_HEADS, nq),
        in_specs=[
            pl.BlockSpec(memory_space=pltpu.SMEM),
            pl.BlockSpec((tq, VAL_DIM), lambda b, h, i: (b * nq + i, h)),
            pl.BlockSpec((seq, VAL_DIM), lambda b, h, i: (b, h)),
            pl.BlockSpec((seq, VAL_DIM), lambda b, h, i: (b, h)),
            pl.BlockSpec((1, VAL_DIM), lambda b, h, i: (0, 0)),
        ],
        out_specs=pl.BlockSpec((tq, VAL_DIM), lambda b, h, i: (b * nq + i, h)),
        out_shape=jax.ShapeDtypeStruct((t_tot, d), BF16),
        scratch_shapes=[pltpu.VMEM((2 * tq, 1), F32), pltpu.VMEM((2 * tq, 1), F32),
                        pltpu.VMEM((2 * tq, VAL_DIM), F32)],
        compiler_params=_cparams("arbitrary", "arbitrary", "arbitrary"),
        name="attn_prompt",
    )(lam, q, k, v, subln_row)


def _attn_s_kernel(pt_ref, lam_ref, q_ref, kn_ref, vn_ref, sub_ref, *rest, n_pages, n_new, out_scale):
    del pt_ref
    k_refs = rest[:n_pages]
    v_refs = rest[n_pages:2 * n_pages]
    o_ref = rest[2 * n_pages]
    r = n_new * N_HEADS
    qs = _split_components(q_ref[...])
    row = lax.broadcasted_iota(jnp.int32, (2 * r, 1), 0)
    row_h = row % N_HEADS
    row_t = (row // N_HEADS) % n_new
    kcols = PAGE_SIZE * N_HEADS
    col_h = lax.broadcasted_iota(jnp.int32, (1, kcols), 1) % N_HEADS
    same_head = row_h == col_h

    m = jnp.full((2 * r, 1), -jnp.inf, F32)
    l = jnp.zeros((2 * r, 1), F32)
    acc = jnp.zeros((2 * r, VAL_DIM), F32)

    def update(m, l, acc, s, v):
        m_new = jnp.maximum(m, jnp.max(s, axis=-1, keepdims=True))
        p = jnp.exp(s - m_new)
        alpha = jnp.exp(m - m_new)
        return m_new, alpha * l + jnp.sum(p, axis=-1, keepdims=True), alpha * acc + _dot(p.astype(BF16), v)

    for p_i in range(n_pages):
        k = k_refs[p_i][...].reshape(kcols, VAL_DIM).astype(BF16)
        v = v_refs[p_i][...].reshape(kcols, VAL_DIM).astype(BF16)
        s = jnp.where(same_head, _dot_nt(qs, k), -jnp.inf)
        m, l, acc = update(m, l, acc, s, v)

    ncol = lax.broadcasted_iota(jnp.int32, (1, r), 1)
    new_ok = (row_h == ncol % N_HEADS) & (ncol // N_HEADS <= row_t)
    s = jnp.where(new_ok, _dot_nt(qs, kn_ref[...]), -jnp.inf)
    m, l, acc = update(m, l, acc, s, vn_ref[...])
    o_ref[...] = _diff_finish(acc, l, lam_ref[0], sub_ref[...], out_scale, r).astype(BF16)


def _attn_sample(q3, kn3, vn3, cache_k, cache_v, page_table, lam, subln_row, j, out_scale):
    n_seq, r, _ = q3.shape
    n_pages = page_table.shape[1]
    n_new = r // N_HEADS
    row3 = pl.BlockSpec((None, r, VAL_DIM), lambda b, pt: (b, 0, 0))

    def page_spec(p):
        return pl.BlockSpec((None, None, PAGE_SIZE, N_HEADS, VAL_DIM), lambda b, pt: (j, pt[b, p], 0, 0, 0))

    grid_spec = pltpu.PrefetchScalarGridSpec(
        num_scalar_prefetch=1,
        grid=(n_seq,),
        in_specs=[pl.BlockSpec(memory_space=pltpu.SMEM), row3, row3, row3,
                  pl.BlockSpec((1, VAL_DIM), lambda b, pt: (0, 0))]
        + [page_spec(p) for p in range(n_pages)] * 2,
        out_specs=row3,
    )
    return pl.pallas_call(
        functools.partial(_attn_s_kernel, n_pages=n_pages, n_new=n_new, out_scale=out_scale),
        grid_spec=grid_spec,
        out_shape=jax.ShapeDtypeStruct((n_seq, r, VAL_DIM), BF16),
        compiler_params=_cparams("arbitrary"),
        name="attn_sample",
    )(page_table, lam, q3, kn3, vn3, subln_row, *([cache_k] * n_pages), *([cache_v] * n_pages))


def _wo_kernel(x_ref, o_ref, gt_ref, w_ref, xo_ref):
    xo_ref[...] = x_ref[...] + gt_ref[...] * _dot(o_ref[...], w_ref[...])


def _attn_out(x, o, grp, layer, wo_bf, tm):
    t_tot, d = x.shape
    tok = pl.BlockSpec((tm, d), lambda t: (t, 0))
    return pl.pallas_call(
        _wo_kernel,
        grid=(t_tot // tm,),
        in_specs=[tok, tok, grp.mod_spec(layer, 2, tm), pl.BlockSpec((d, d), lambda t: (0, 0))],
        out_specs=tok,
        out_shape=jax.ShapeDtypeStruct((t_tot, d), F32),
        compiler_params=_cparams("arbitrary"),
        name="attn_out",
    )(x, o, grp.mods, wo_bf)


def _route(logits_t, bias):
    s_all = jax.nn.sigmoid(logits_t)
    sb_all = s_all + bias
    s = [s_all[e:e + 1, :] for e in range(N_EXPERTS)]
    sb = [sb_all[e:e + 1, :] for e in range(N_EXPERTS)]
    n = EXPERTS_PER_GROUP
    gscore = []
    for g in range(N_EXPERT_GROUPS):
        v = sb[g * n:(g + 1) * n]
        best = v[0] + v[1]
        for a in range(n):
            for b in range(a + 1, n):
                if (a, b) != (0, 1):
                    best = jnp.maximum(best, v[a] + v[b])
        gscore.append(best)
    gsel = jnp.zeros_like(gscore[0], dtype=jnp.int32)
    gbest = gscore[0]
    for g in range(1, N_EXPERT_GROUPS):
        better = gscore[g] > gbest
        gsel = jnp.where(better, g, gsel)
        gbest = jnp.where(better, gscore[g], gbest)

    def pick(rows, r):
        out = rows[r]
        for g in range(1, N_EXPERT_GROUPS):
            out = jnp.where(gsel == g, rows[g * n + r], out)
        return out

    vb = [pick(sb, r) for r in range(n)]
    vu = [pick(s, r) for r in range(n)]

    def argmax_first(vals):
        idx = jnp.zeros_like(gsel)
        best = vals[0]
        for r in range(1, n):
            better = vals[r] > best
            idx = jnp.where(better, r, idx)
            best = jnp.where(better, vals[r], best)
        return idx

    i1 = argmax_first(vb)
    i2 = argmax_first([jnp.where(i1 == r, -jnp.inf, vb[r]) for r in range(n)])

    def take(vals, idx):
        out = vals[0]
        for r in range(1, n):
            out = jnp.where(idx == r, vals[r], out)
        return out

    w1 = take(vu, i1)
    w2 = take(vu, i2)
    tot = w1 + w2
    e1 = gsel * n + i1
    e2 = gsel * n + i2
    erow = lax.broadcasted_iota(jnp.int32, logits_t.shape, 0)
    return jnp.where(erow == e1, w1 / tot, 0.0) + jnp.where(erow == e2, w2 / tot, 0.0)


def _router_logits(h2, h2_bf, rw):
    rw_hi = rw.astype(BF16)
    rw_lo = (rw - rw_hi.astype(F32)).astype(BF16)
    h_lo = (h2 - h2_bf.astype(F32)).astype(BF16)
    return _dot_nt(rw_hi, h2_bf) + (_dot_nt(rw_hi, h_lo) + _dot_nt(rw_lo, h2_bf))


def _moe_dense_kernel(x_ref, sh_ref, sc_ref, gt_ref, nrm_ref, rw_ref, rb_ref, wg_ref, wu_ref, wd_ref,
                      xo_ref, h_ref, gcol_ref, acc_ref, *, tm):
    e = pl.program_id(1)

    @pl.when(e == 0)
    def _():
        h2 = _normmod(x_ref[...], nrm_ref[...], sc_ref[...], sh_ref[...])
        h2_bf = h2.astype(BF16)
        h_ref[...] = h2_bf
        gates_t = _route(_router_logits(h2, h2_bf, rw_ref[...]), rb_ref[...])
        pad = jnp.zeros((128 - N_EXPERTS, tm), F32)
        gates = jnp.concatenate([gates_t, pad], axis=0).T
        for ee in range(N_EXPERTS):
            gcol_ref[ee] = gates[:, ee:ee + 1]
        acc_ref[...] = jnp.zeros(acc_ref.shape, F32)

    h = h_ref[...]
    hg = _dot(h, wg_ref[...])
    hu = _dot(h, wu_ref[...])
    act = (hg * jax.nn.sigmoid(hg)) * hu * gcol_ref[e]
    acc_ref[...] += _dot(act.astype(BF16), wd_ref[...])

    @pl.when(e == N_EXPERTS - 1)
    def _():
        xo_ref[...] = x_ref[...] + gt_ref[...] * acc_ref[...]


def _moe_dense(x, grp, layer, norm_ffn, router_wt, router_b, wg_bf, wu_bf, wd_bf, tm):
    t_tot, d = x.shape
    tok = pl.BlockSpec((tm, d), lambda t, e: (t, 0))
    return pl.pallas_call(
        functools.partial(_moe_dense_kernel, tm=tm),
        grid=(t_tot // tm, N_EXPERTS),
        in_specs=[tok, grp.mod_spec(layer, 3, tm), grp.mod_spec(layer, 4, tm), grp.mod_spec(layer, 5, tm),
                  _row_spec(layer),
                  pl.BlockSpec((N_EXPERTS, d), lambda t, e: (0, 0)),
                  pl.BlockSpec((N_EXPERTS, 1), lambda t, e: (0, 0)),
                  pl.BlockSpec((None, None, d, D_EXPERT), lambda t, e: (layer, e, 0, 0)),
                  pl.BlockSpec((None, None, d, D_EXPERT), lambda t, e: (layer, e, 0, 0)),
                  pl.BlockSpec((None, None, D_EXPERT, d), lambda t, e: (layer, e, 0, 0))],
        out_specs=tok,
        out_shape=jax.ShapeDtypeStruct((t_tot, d), F32),
        scratch_shapes=[pltpu.VMEM((tm, d), BF16), pltpu.VMEM((N_EXPERTS, tm, 1), F32),
                        pltpu.VMEM((tm, d), F32)],
        compiler_params=_cparams("arbitrary", "arbitrary"),
        name="moe_dense",
    )(x, grp.mods, grp.mods, grp.mods, norm_ffn, router_wt, router_b, wg_bf, wu_bf, wd_bf)


def _final_kernel(x_ref, g_ref, o_ref):
    x = x_ref[...]
    o_ref[...] = (x * lax.rsqrt(jnp.mean(x * x, axis=-1, keepdims=True) + EPS)) * g_ref[...]


def _final_norm(x, g_row, tm):
    t_tot, d = x.shape
    tok = pl.BlockSpec((tm, d), lambda t: (t, 0))
    return pl.pallas_call(
        _final_kernel,
        grid=(t_tot // tm,),
        in_specs=[tok, pl.BlockSpec((1, d), lambda t: (0, 0))],
        out_specs=tok,
        out_shape=jax.ShapeDtypeStruct((t_tot, d), F32),
        compiler_params=_cparams("arbitrary"),
        name="final_norm",
    )(x, g_row)


def kernel(x_prompt, x_sample, c_prompt, c_sample, state_pool, cache_k, cache_v, page_table, ada_w, ada_b, norm_mix, norm_ffn, norm_final, pool_w, pool_scale, attn_w_qkv, attn_w_o, lambda_q1, lambda_k1, lambda_q2, lambda_k2, subln, router_w, router_bias, moe_w_gate, moe_w_up, moe_w_down):
    bp, n_p, d = x_prompt.shape
    bs, n_s, _ = x_sample.shape
    depth = ada_w.shape[0]
    past_len = page_table.shape[1] * PAGE_SIZE
    tm_p, tm_s = 512, 256

    mods_s, mods_p = _ada_mods(c_sample, c_prompt, ada_w, ada_b)
    grp_p = _Group(mods_p.reshape(depth, bp, 1, N_MODS * d), per_token=False, seq=n_p)
    grp_s = _Group(jnp.repeat(mods_s, n_s, axis=1), per_token=True, seq=n_s)

    xp = x_prompt.reshape(bp * n_p, d)
    xs = x_sample.reshape(bs * n_s, d)

    norm_mix3 = norm_mix.reshape(depth, 1, d)
    norm_ffn3 = norm_ffn.reshape(depth, 1, d)
    pool_w_bf = pool_w.astype(BF16)
    pool_scale3 = pool_scale.reshape(-1, 1, d)
    wqkv_bf = attn_w_qkv.astype(BF16)
    wo_bf = attn_w_o.astype(BF16)
    router_wt = router_w.T
    router_b = router_bias.reshape(N_EXPERTS, 1)
    wg_bf = moe_w_gate.astype(BF16)
    wu_bf = moe_w_up.astype(BF16)
    wd_bf = moe_w_down.astype(BF16)

    cos_p, sin_p = _rope_tables(jnp.arange(n_p))
    cos_s, sin_s = _rope_tables(past_len + jnp.arange(n_s))
    cos_s = jnp.tile(cos_s, (tm_s // n_s, 1))
    sin_s = jnp.tile(sin_s, (tm_s // n_s, 1))

    pool_p, pool_s, kp_rows, vp_rows, ks_rows, vs_rows = [], [], [], [], [], []
    for i in range(depth):
        j = i // 2
        if i % 2 == 0:
            xp, st_p = _pool_prompt(xp, grp_p, i, norm_mix3, pool_w_bf, pool_scale3, j, bp, ts=tm_p)
            state_pad = jnp.pad(state_pool[j], ((0, 0), (1, 0), (0, 0)))
            xs, st_s = _pool_sample(xs, state_pad, grp_s, i, norm_mix3, pool_w_bf, pool_scale3, j, n_s)
            pool_p.append(st_p[:, 1:])
            pool_s.append(st_s[:, 1:])
        else:
            lam_init = 0.8 - 0.6 * math.exp(-0.3 * i)
            lam = (jnp.exp(jnp.sum(lambda_q1[j] * lambda_k1[j])) - jnp.exp(jnp.sum(lambda_q2[j] * lambda_k2[j]))
                   + lam_init).reshape(1).astype(F32)
            sub_row = subln[j].reshape(1, VAL_DIM)
            out_scale = 1.0 - lam_init

            q, kb, vb, kf, vf = _qkv(xp, grp_p, i, norm_mix3, wqkv_bf[j], cos_p, sin_p, tm_p)
            o = _attn_prompt(q, kb, vb, lam, sub_row, bp, n_p, out_scale)
            xp = _attn_out(xp, o, grp_p, i, wo_bf[j], tm_p)
            kp_rows.append(kf.reshape(bp, n_p, N_HEADS, VAL_DIM))
            vp_rows.append(vf.reshape(bp, n_p, N_HEADS, VAL_DIM))

            q, kb, vb, kf, vf = _qkv(xs, grp_s, i, norm_mix3, wqkv_bf[j], cos_s, sin_s, tm_s)
            r = n_s * N_HEADS
            o3 = _attn_sample(q.reshape(bs, r, VAL_DIM), kb.reshape(bs, r, VAL_DIM), vb.reshape(bs, r, VAL_DIM),
                              cache_k, cache_v, page_table, lam, sub_row, j, out_scale)
            xs = _attn_out(xs, o3.reshape(bs * n_s, d), grp_s, i, wo_bf[j], tm_s)
            ks_rows.append(kf.reshape(bs, n_s, N_HEADS, VAL_DIM))
            vs_rows.append(vf.reshape(bs, n_s, N_HEADS, VAL_DIM))

        xp = _moe_dense(xp, grp_p, i, norm_ffn3, router_wt, router_b, wg_bf, wu_bf, wd_bf, tm_p)
        xs = _moe_dense(xs, grp_s, i, norm_ffn3, router_wt, router_b, wg_bf, wu_bf, wd_bf, tm_s)

    g_row = norm_final.reshape(1, d)
    y_prompt = _final_norm(xp, g_row, tm_p).reshape(bp, n_p, d)
    y_sample = _final_norm(xs, g_row, tm_s).reshape(bs, n_s, d)
    return (y_prompt, y_sample, jnp.stack(pool_p), jnp.stack(pool_s),
            jnp.stack(kp_rows), jnp.stack(vp_rows), jnp.stack(ks_rows), jnp.stack(vs_rows))
```

```python
import functools
import math

import jax
import jax.numpy as jnp
from jax import lax
from jax.experimental import pallas as pl
from jax.experimental.pallas import tpu as pltpu

F32 = jnp.float32
BF16 = jnp.bfloat16

D_MODEL = 1024
N_HEADS = 8
HEAD_DIM = 64
VAL_DIM = 128
ROPE_THETA = 10000.0
POOL_WINDOWS = (2, 4, 8, 16)
GROUP_CH = 256
MAX_WIN = 16
N_EXPERTS = 16
EXPERTS_PER_GROUP = 4
N_EXPERT_GROUPS = 4
D_EXPERT = 256
PAGE_SIZE = 128
EPS = 1e-6
N_MODS = 6

VMEM_LIMIT_BYTES = 52 * 1024 * 1024


def _cparams(*sem):
    return pltpu.CompilerParams(dimension_semantics=sem, vmem_limit_bytes=VMEM_LIMIT_BYTES)


def _normmod(x, g, sc, sh):
    r = lax.rsqrt(jnp.mean(x * x, axis=-1, keepdims=True) + EPS)
    return (x * r) * g * (1.0 + sc) + sh


def _dot(a, b):
    return jnp.dot(a, b, preferred_element_type=F32)


def _dot_nt(a, b):
    return lax.dot_general(a, b, (((1,), (1,)), ((), ())), preferred_element_type=F32)


def _ada_kernel(c_ref, w_ref, b_ref, os_ref, op_ref, *, n_s):
    c = c_ref[...]
    a = (c * jax.nn.sigmoid(c)).astype(BF16)
    r = _dot(a, w_ref[...].astype(BF16)) + b_ref[...]
    os_ref[...] = r[:n_s]
    op_ref[...] = r[n_s:]


def _ada_mods(c_sample, c_prompt, ada_w, ada_b, tn=768):
    depth, d, n = ada_w.shape
    n_s, n_p = c_sample.shape[0], c_prompt.shape[0]
    c_all = jnp.concatenate([c_sample, c_prompt], axis=0)
    return pl.pallas_call(
        functools.partial(_ada_kernel, n_s=n_s),
        grid=(depth, n // tn),
        in_specs=[
            pl.BlockSpec((n_s + n_p, d), lambda i, j: (0, 0)),
            pl.BlockSpec((None, d, tn), lambda i, j: (i, 0, j)),
            pl.BlockSpec((None, 1, tn), lambda i, j: (i, 0, j)),
        ],
        out_specs=[
            pl.BlockSpec((None, n_s, tn), lambda i, j: (i, 0, j)),
            pl.BlockSpec((None, n_p, tn), lambda i, j: (i, 0, j)),
        ],
        out_shape=[jax.ShapeDtypeStruct((depth, n_s, n), F32),
                   jax.ShapeDtypeStruct((depth, n_p, n), F32)],
        compiler_params=_cparams("arbitrary", "arbitrary"),
        name="ada_mods",
    )(c_all, ada_w, ada_b.reshape(depth, 1, n))


class _Group:
    def __init__(self, mods, per_token, seq):
        self.mods = mods
        self.per_token = per_token
        self.seq = seq

    def mod_spec(self, layer, chunk, tm):
        if self.per_token:
            return pl.BlockSpec((None, tm, D_MODEL), lambda t, *_: (layer, t, chunk))
        tpb = self.seq // tm
        return pl.BlockSpec((None, None, 1, D_MODEL), lambda t, *_: (layer, t // tpb, 0, chunk))


def _row_spec(layer, width=D_MODEL):
    return pl.BlockSpec((None, 1, width), lambda t, *_: (layer, 0, 0))


def _pool_p_kernel(x_ref, halo_ref, sh_ref, sc_ref, gt_ref, nrm_ref, w_ref, ps_ref,
                   xo_ref, st_ref, ext_ref, *, ts, tpb, rc):
    s = pl.program_id(0) % tpb
    g = nrm_ref[...]
    sc = sc_ref[...]
    sh = sh_ref[...]
    hh = _normmod(halo_ref[...], g, sc, sh)
    ext_ref[0:MAX_WIN, :] = jnp.where(s == 0, 0.0, hh)
    ext_ref[MAX_WIN:, :] = _normmod(x_ref[...], g, sc, sh)
    for r0 in range(0, ts, rc):
        pos = s * ts + r0 + lax.broadcasted_iota(jnp.int32, (rc, 1), 0)
        for gi, w in enumerate(POOL_WINDOWS):
            cs = slice(gi * GROUP_CH, (gi + 1) * GROUP_CH)
            base = MAX_WIN + r0
            h = ext_ref[base:base + rc, cs]
            acc = h
            for k in range(1, w):
                acc = acc + ext_ref[base - k:base - k + rc, cs]
            cnt = jnp.minimum(w, pos + 1).astype(F32)
            pooled = acc / cnt - h
            o = _dot(pooled.astype(BF16), w_ref[gi]) * ps_ref[:, cs]
            xo_ref[r0:r0 + rc, cs] = x_ref[r0:r0 + rc, cs] + gt_ref[:, cs] * o

    @pl.when(s == tpb - 1)
    def _():
        st_ref[...] = ext_ref[ts:ts + MAX_WIN, :]


def _pool_prompt(x, grp, layer, norm_mix, pool_w_bf, pool_scale, j, n_batch, ts=512, rc=128):
    t_tot, d = x.shape
    tpb = grp.seq // ts
    hb = ts // MAX_WIN
    return pl.pallas_call(
        functools.partial(_pool_p_kernel, ts=ts, tpb=tpb, rc=rc),
        grid=(t_tot // ts,),
        in_specs=[
            pl.BlockSpec((ts, d), lambda t: (t, 0)),
            pl.BlockSpec((MAX_WIN, d), lambda t: (jnp.maximum(t * hb - 1, 0), 0)),
            grp.mod_spec(layer, 0, ts), grp.mod_spec(layer, 1, ts), grp.mod_spec(layer, 2, ts),
            _row_spec(layer),
            pl.BlockSpec((None, 4, GROUP_CH, GROUP_CH), lambda t: (j, 0, 0, 0)),
            _row_spec(j),
        ],
        out_specs=[
            pl.BlockSpec((ts, d), lambda t: (t, 0)),
            pl.BlockSpec((None, MAX_WIN, d), lambda t: (t // tpb, 0, 0)),
        ],
        out_shape=[jax.ShapeDtypeStruct((t_tot, d), F32),
                   jax.ShapeDtypeStruct((n_batch, MAX_WIN, d), F32)],
        scratch_shapes=[pltpu.VMEM((ts + MAX_WIN, d), F32)],
        compiler_params=_cparams("arbitrary"),
        name="pool_prompt",
    )(x, x, grp.mods, grp.mods, grp.mods, norm_mix, pool_w_bf, pool_scale)


def _pool_s_kernel(x_ref, st_ref, sh_ref, sc_ref, gt_ref, nrm_ref, w_ref, ps_ref,
                   xo_ref, so_ref, ext_ref, *, bb, n_new):
    x = x_ref[...]
    h2 = _normmod(x, nrm_ref[...], sc_ref[...], sh_ref[...])
    ext_ref[:, 0:MAX_WIN, :] = st_ref[...]
    ext_ref[:, MAX_WIN:, :] = h2.reshape(bb, n_new, D_MODEL)
    outs = []
    for gi, w in enumerate(POOL_WINDOWS):
        cs = slice(gi * GROUP_CH, (gi + 1) * GROUP_CH)
        h = ext_ref[:, MAX_WIN:MAX_WIN + n_new, cs]
        acc = h
        for k in range(1, w):
            acc = acc + ext_ref[:, MAX_WIN - k:MAX_WIN - k + n_new, cs]
        pooled = (acc / float(w) - h).reshape(bb * n_new, GROUP_CH)
        outs.append(_dot(pooled.astype(BF16), w_ref[gi]))
    out = jnp.concatenate(outs, axis=-1) * ps_ref[...]
    xo_ref[...] = x + gt_ref[...] * out
    so_ref[...] = ext_ref[:, n_new:n_new + MAX_WIN, :]


def _pool_sample(x, state_pad, grp, layer, norm_mix, pool_w_bf, pool_scale, j, n_new, bb=32):
    t_tot, d = x.shape
    n_seq = t_tot // n_new
    tm = bb * n_new
    return pl.pallas_call(
        functools.partial(_pool_s_kernel, bb=bb, n_new=n_new),
        grid=(n_seq // bb,),
        in_specs=[
            pl.BlockSpec((tm, d), lambda t: (t, 0)),
            pl.BlockSpec((bb, MAX_WIN, d), lambda t: (t, 0, 0)),
            grp.mod_spec(layer, 0, tm), grp.mod_spec(layer, 1, tm), grp.mod_spec(layer, 2, tm),
            _row_spec(layer),
            pl.BlockSpec((None, 4, GROUP_CH, GROUP_CH), lambda t: (j, 0, 0, 0)),
            _row_spec(j),
        ],
        out_specs=[
            pl.BlockSpec((tm, d), lambda t: (t, 0)),
            pl.BlockSpec((bb, MAX_WIN, d), lambda t: (t, 0, 0)),
        ],
        out_shape=[jax.ShapeDtypeStruct((t_tot, d), F32),
                   jax.ShapeDtypeStruct((n_seq, MAX_WIN, d), F32)],
        scratch_shapes=[pltpu.VMEM((bb, MAX_WIN + n_new, d), F32)],
        compiler_params=_cparams("arbitrary"),
        name="pool_sample",
    )(x, state_pad, grp.mods, grp.mods, grp.mods, norm_mix, pool_w_bf, pool_scale)


def _qkv_kernel(x_ref, sh_ref, sc_ref, nrm_ref, w_ref, cos_ref, sin_ref,
                q_ref, kb_ref, vb_ref, kf_ref, vf_ref):
    h = _normmod(x_ref[...], nrm_ref[...], sc_ref[...], sh_ref[...]).astype(BF16)
    cos = cos_ref[...]
    sin = sin_ref[...]
    lane = lax.broadcasted_iota(jnp.int32, (1, VAL_DIM), 1)
    first_half = (lane % HEAD_DIM) < (HEAD_DIM // 2)

    def rope(xh):
        rot = jnp.where(first_half, pltpu.roll(xh, VAL_DIM - HEAD_DIM // 2, 1),
                        pltpu.roll(xh, HEAD_DIM // 2, 1))
        return xh * cos + rot * sin

    pw = 2 * VAL_DIM
    for c0 in range(0, D_MODEL, pw):
        q2 = _dot(h, w_ref[:, c0:c0 + pw])
        k2 = _dot(h, w_ref[:, D_MODEL + c0:D_MODEL + c0 + pw])
        v2 = _dot(h, w_ref[:, 2 * D_MODEL + c0:2 * D_MODEL + c0 + pw])
        for u in range(2):
            cs = slice(c0 + u * VAL_DIM, c0 + (u + 1) * VAL_DIM)
            us = slice(u * VAL_DIM, (u + 1) * VAL_DIM)
            q_ref[:, cs] = (rope(q2[:, us]) * (HEAD_DIM ** -0.5)).astype(BF16)
            kr = rope(k2[:, us])
            kf_ref[:, cs] = kr
            kb_ref[:, cs] = kr.astype(BF16)
        vf_ref[:, c0:c0 + pw] = v2
        vb_ref[:, c0:c0 + pw] = v2.astype(BF16)


def _qkv(x, grp, layer, norm_mix, wqkv_bf, cos_tab, sin_tab, tm):
    t_tot, d = x.shape
    n_pos_tiles = cos_tab.shape[0] // tm
    tok = pl.BlockSpec((tm, d), lambda t: (t, 0))
    tab = pl.BlockSpec((tm, VAL_DIM), lambda t: (t % n_pos_tiles, 0))
    return pl.pallas_call(
        _qkv_kernel,
        grid=(t_tot // tm,),
        in_specs=[tok, grp.mod_spec(layer, 0, tm), grp.mod_spec(layer, 1, tm), _row_spec(layer),
                  pl.BlockSpec((d, 3 * d), lambda t: (0, 0)), tab, tab],
        out_specs=[tok] * 5,
        out_shape=[jax.ShapeDtypeStruct((t_tot, d), BF16)] * 3 + [jax.ShapeDtypeStruct((t_tot, d), F32)] * 2,
        compiler_params=_cparams("arbitrary"),
        name="qkv_rope",
    )(x, grp.mods, grp.mods, norm_mix, wqkv_bf, cos_tab, sin_tab)


def _rope_tables(pos):
    inv = 1.0 / (ROPE_THETA ** (jnp.arange(0, HEAD_DIM, 2, dtype=F32) / HEAD_DIM))
    ang = pos.astype(F32)[:, None] * inv[None, :]
    cos, sin = jnp.cos(ang), jnp.sin(ang)
    cos_t = jnp.concatenate([cos, cos, cos, cos], axis=-1)
    sin_t = jnp.concatenate([-sin, sin, -sin, sin], axis=-1)
    return cos_t, sin_t


def _split_components(q):
    lane = lax.broadcasted_iota(jnp.int32, (1, VAL_DIM), 1)
    zero = jnp.zeros_like(q)
    return jnp.concatenate([jnp.where(lane < HEAD_DIM, q, zero), jnp.where(lane >= HEAD_DIM, q, zero)], axis=0)


def _with_ones(v):
    return jnp.concatenate([v, jnp.ones_like(v)], axis=1)


def _online_update(s, m_old, acc_old, v1):
    chunks = [s[:, u:u + VAL_DIM] for u in range(0, s.shape[1], VAL_DIM)]
    cm = chunks[0]
    for c in chunks[1:]:
        cm = jnp.maximum(cm, c)
    m_new = jnp.maximum(m_old, jnp.max(cm, axis=-1, keepdims=True))
    alpha = jnp.exp(m_old - m_new)
    p = jnp.concatenate([jnp.exp(c - m_new) for c in chunks], axis=1).astype(BF16)
    acc_new = jnp.concatenate([alpha, alpha], axis=1) * acc_old + _dot(p, v1)
    return m_new, acc_new


def _diff_finish(acc, lam, subln, out_scale, r):
    o = acc[:, :VAL_DIM] / acc[:, VAL_DIM:]
    od = o[:r] - lam * o[r:]
    y = od * lax.rsqrt(jnp.mean(od * od, axis=-1, keepdims=True) + EPS)
    return y * subln * out_scale


def _attn_p_kernel(lam_ref, q_ref, k_ref, v_ref, sub_ref, o_ref, qs_ref, m_ref, acc_ref, *, tq, rc, out_scale):
    qi = pl.program_id(2)
    r = 2 * tq
    qs_ref[...] = _split_components(q_ref[...])
    m_ref[...] = jnp.full(m_ref.shape, -jnp.inf, F32)
    acc_ref[...] = jnp.zeros(acc_ref.shape, F32)

    def step(kj, masked):
        off = pl.multiple_of(kj * tq, tq)
        k = k_ref[pl.ds(off, tq), :]
        v1 = _with_ones(v_ref[pl.ds(off, tq), :])
        for r0 in range(0, r, rc):
            rows = slice(r0, r0 + rc)
            nk = (r0 % tq) + rc if masked else tq
            s = _dot_nt(qs_ref[rows, :], k[:nk])
            if masked:
                row = (r0 + lax.broadcasted_iota(jnp.int32, (rc, 1), 0)) % tq
                col = lax.broadcasted_iota(jnp.int32, (1, nk), 1)
                s = jnp.where(col <= row, s, -jnp.inf)
            m_new, acc_new = _online_update(s, m_ref[rows, :], acc_ref[rows, :], v1[:nk])
            m_ref[rows, :] = m_new
            acc_ref[rows, :] = acc_new

    def body(t, carry):
        step(2 * t, False)
        step(2 * t + 1, False)
        return carry

    lax.fori_loop(0, qi // 2, body, 0)

    @pl.when(qi % 2 == 1)
    def _():
        step(qi - 1, False)

    step(qi, True)
    o_ref[...] = _diff_finish(acc_ref[...], lam_ref[0], sub_ref[...], out_scale, tq).astype(BF16)


def _attn_prompt(q, k, v, lam, subln_row, n_batch, seq, out_scale, tq=512, rc=256):
    t_tot, d = q.shape
    nq = seq // tq
    return pl.pallas_call(
        functools.partial(_attn_p_kernel, tq=tq, rc=rc, out_scale=out_scale),
        grid=(n_batch, N_HEADS, nq),
        in_specs=[
            pl.BlockSpec(memory_space=pltpu.SMEM),
            pl.BlockSpec((tq, VAL_DIM), lambda b, h, i: (b * nq + i, h)),
            pl.BlockSpec((seq, VAL_DIM), lambda b, h, i: (b, h)),
            pl.BlockSpec((seq, VAL_DIM), lambda b, h, i: (b, h)),
            pl.BlockSpec((1, VAL_DIM), lambda b, h, i: (0, 0)),
        ],
        out_specs=pl.BlockSpec((tq, VAL_DIM), lambda b, h, i: (b * nq + i, h)),
        out_shape=jax.ShapeDtypeStruct((t_tot, d), BF16),
        scratch_shapes=[pltpu.VMEM((2 * tq, VAL_DIM), BF16), pltpu.VMEM((2 * tq, VAL_DIM), F32),
                        pltpu.VMEM((2 * tq, 2 * VAL_DIM), F32)],
        compiler_params=_cparams("arbitrary", "arbitrary", "arbitrary"),
        name="attn_prompt",
    )(lam, q, k, v, subln_row)


def _attn_s_kernel(pt_ref, lam_ref, q_ref, kn_ref, vn_ref, sub_ref, *rest, n_pages, n_new, out_scale):
    del pt_ref
    k_refs = rest[:n_pages]
    v_refs = rest[n_pages:2 * n_pages]
    o_ref = rest[2 * n_pages]
    r = n_new * N_HEADS
    qs = _split_components(q_ref[...])
    row = lax.broadcasted_iota(jnp.int32, (2 * r, 1), 0)
    row_h = row % N_HEADS
    row_t = (row // N_HEADS) % n_new
    kcols = PAGE_SIZE * N_HEADS
    col_h = lax.broadcasted_iota(jnp.int32, (1, kcols), 1) % N_HEADS
    same_head = row_h == col_h

    m = jnp.full((2 * r, VAL_DIM), -jnp.inf, F32)
    acc = jnp.zeros((2 * r, 2 * VAL_DIM), F32)
    for p_i in range(n_pages):
        k = k_refs[p_i][...].reshape(kcols, VAL_DIM).astype(BF16)
        v = v_refs[p_i][...].reshape(kcols, VAL_DIM).astype(BF16)
        s = jnp.where(same_head, _dot_nt(qs, k), -jnp.inf)
        m, acc = _online_update(s, m, acc, _with_ones(v))

    ncol = lax.broadcasted_iota(jnp.int32, (1, VAL_DIM), 1)
    new_ok = (ncol < r) & (row_h == ncol % N_HEADS) & (ncol // N_HEADS <= row_t)
    pad = jnp.zeros((VAL_DIM - r, VAL_DIM), BF16)
    kn = jnp.concatenate([kn_ref[...], pad], axis=0)
    vn = jnp.concatenate([vn_ref[...], pad], axis=0)
    s = jnp.where(new_ok, _dot_nt(qs, kn), -jnp.inf)
    m, acc = _online_update(s, m, acc, _with_ones(vn))
    o_ref[...] = _diff_finish(acc, lam_ref[0], sub_ref[...], out_scale, r).astype(BF16)


def _attn_sample(q3, kn3, vn3, cache_k, cache_v, page_table, lam, subln_row, j, out_scale):
    n_seq, r, _ = q3.shape
    n_pages = page_table.shape[1]
    n_new = r // N_HEADS
    row3 = pl.BlockSpec((None, r, VAL_DIM), lambda b, pt: (b, 0, 0))

    def page_spec(p):
        return pl.BlockSpec((None, None, PAGE_SIZE, N_HEADS, VAL_DIM), lambda b, pt: (j, pt[b, p], 0, 0, 0))

    grid_spec = pltpu.PrefetchScalarGridSpec(
        num_scalar_prefetch=1,
        grid=(n_seq,),
        in_specs=[pl.BlockSpec(memory_space=pltpu.SMEM), row3, row3, row3,
                  pl.BlockSpec((1, VAL_DIM), lambda b, pt: (0, 0))]
        + [page_spec(p) for p in range(n_pages)] * 2,
        out_specs=row3,
    )
    return pl.pallas_call(
        functools.partial(_attn_s_kernel, n_pages=n_pages, n_new=n_new, out_scale=out_scale),
        grid_spec=grid_spec,
        out_shape=jax.ShapeDtypeStruct((n_seq, r, VAL_DIM), BF16),
        compiler_params=_cparams("arbitrary"),
        name="attn_sample",
    )(page_table, lam, q3, kn3, vn3, subln_row, *([cache_k] * n_pages), *([cache_v] * n_pages))


def _wo_kernel(x_ref, o_ref, gt_ref, w_ref, xo_ref):
    xo_ref[...] = x_ref[...] + gt_ref[...] * _dot(o_ref[...], w_ref[...])


def _attn_out(x, o, grp, layer, wo_bf, tm):
    t_tot, d = x.shape
    tok = pl.BlockSpec((tm, d), lambda t: (t, 0))
    return pl.pallas_call(
        _wo_kernel,
        grid=(t_tot // tm,),
        in_specs=[tok, tok, grp.mod_spec(layer, 2, tm), pl.BlockSpec((d, d), lambda t: (0, 0))],
        out_specs=tok,
        out_shape=jax.ShapeDtypeStruct((t_tot, d), F32),
        compiler_params=_cparams("arbitrary"),
        name="attn_out",
    )(x, o, grp.mods, wo_bf)


def _route(logits_t, bias):
    s_all = jax.nn.sigmoid(logits_t)
    sb_all = s_all + bias
    s = [s_all[e:e + 1, :] for e in range(N_EXPERTS)]
    sb = [sb_all[e:e + 1, :] for e in range(N_EXPERTS)]
    n = EXPERTS_PER_GROUP
    gscore = []
    for g in range(N_EXPERT_GROUPS):
        v = sb[g * n:(g + 1) * n]
        best = v[0] + v[1]
        for a in range(n):
            for b in range(a + 1, n):
                if (a, b) != (0, 1):
                    best = jnp.maximum(best, v[a] + v[b])
        gscore.append(best)
    gsel = jnp.zeros_like(gscore[0], dtype=jnp.int32)
    gbest = gscore[0]
    for g in range(1, N_EXPERT_GROUPS):
        better = gscore[g] > gbest
        gsel = jnp.where(better, g, gsel)
        gbest = jnp.where(better, gscore[g], gbest)

    def pick(rows, r):
        out = rows[r]
        for g in range(1, N_EXPERT_GROUPS):
            out = jnp.where(gsel == g, rows[g * n + r], out)
        return out

    vb = [pick(sb, r) for r in range(n)]
    vu = [pick(s, r) for r in range(n)]

    def argmax_first(vals):
        idx = jnp.zeros_like(gsel)
        best = vals[0]
        for r in range(1, n):
            better = vals[r] > best
            idx = jnp.where(better, r, idx)
            best = jnp.where(better, vals[r], best)
        return idx

    i1 = argmax_first(vb)
    i2 = argmax_first([jnp.where(i1 == r, -jnp.inf, vb[r]) for r in range(n)])

    def take(vals, idx):
        out = vals[0]
        for r in range(1, n):
            out = jnp.where(idx == r, vals[r], out)
        return out

    w1 = take(vu, i1)
    w2 = take(vu, i2)
    tot = w1 + w2
    e1 = gsel * n + i1
    e2 = gsel * n + i2
    erow = lax.broadcasted_iota(jnp.int32, logits_t.shape, 0)
    return jnp.where(erow == e1, w1 / tot, 0.0) + jnp.where(erow == e2, w2 / tot, 0.0)


def _router_logits(h2, h2_bf, rw):
    rw_hi = rw.astype(BF16)
    rw_lo = (rw - rw_hi.astype(F32)).astype(BF16)
    h_lo = (h2 - h2_bf.astype(F32)).astype(BF16)
    return _dot_nt(rw_hi, h2_bf) + (_dot_nt(rw_hi, h_lo) + _dot_nt(rw_lo, h2_bf))


def _moe_dense_kernel(x_ref, sh_ref, sc_ref, gt_ref, nrm_ref, rw_ref, rb_ref, wg_ref, wu_ref, wd_ref,
                      xo_ref, h_ref, gcol_ref, acc_ref, *, tm):
    e = pl.program_id(1)

    @pl.when(e == 0)
    def _():
        h2 = _normmod(x_ref[...], nrm_ref[...], sc_ref[...], sh_ref[...])
        h2_bf = h2.astype(BF16)
        h_ref[...] = h2_bf
        gates_t = _route(_router_logits(h2, h2_bf, rw_ref[...]), rb_ref[...])
        pad = jnp.zeros((128 - N_EXPERTS, tm), F32)
        gates = jnp.concatenate([gates_t, pad], axis=0).T
        for ee in range(N_EXPERTS):
            gcol_ref[ee] = gates[:, ee:ee + 1]
        acc_ref[...] = jnp.zeros(acc_ref.shape, F32)

    h = h_ref[...]
    hg = _dot(h, wg_ref[...])
    hu = _dot(h, wu_ref[...])
    act = (hg * jax.nn.sigmoid(hg)) * hu * gcol_ref[e]
    acc_ref[...] += _dot(act.astype(BF16), wd_ref[...])

    @pl.when(e == N_EXPERTS - 1)
    def _():
        xo_ref[...] = x_ref[...] + gt_ref[...] * acc_ref[...]


def _moe_dense(x, grp, layer, norm_ffn, router_wt, router_b, wg_bf, wu_bf, wd_bf, tm):
    t_tot, d = x.shape
    tok = pl.BlockSpec((tm, d), lambda t, e: (t, 0))
    return pl.pallas_call(
        functools.partial(_moe_dense_kernel, tm=tm),
        grid=(t_tot // tm, N_EXPERTS),
        in_specs=[tok, grp.mod_spec(layer, 3, tm), grp.mod_spec(layer, 4, tm), grp.mod_spec(layer, 5, tm),
                  _row_spec(layer),
                  pl.BlockSpec((N_EXPERTS, d), lambda t, e: (0, 0)),
                  pl.BlockSpec((N_EXPERTS, 1), lambda t, e: (0, 0)),
                  pl.BlockSpec((None, None, d, D_EXPERT), lambda t, e: (layer, e, 0, 0)),
                  pl.BlockSpec((None, None, d, D_EXPERT), lambda t, e: (layer, e, 0, 0)),
                  pl.BlockSpec((None, None, D_EXPERT, d), lambda t, e: (layer, e, 0, 0))],
        out_specs=tok,
        out_shape=jax.ShapeDtypeStruct((t_tot, d), F32),
        scratch_shapes=[pltpu.VMEM((tm, d), BF16), pltpu.VMEM((N_EXPERTS, tm, 1), F32),
                        pltpu.VMEM((tm, d), F32)],
        compiler_params=_cparams("arbitrary", "arbitrary"),
        name="moe_dense",
    )(x, grp.mods, grp.mods, grp.mods, norm_ffn, router_wt, router_b, wg_bf, wu_bf, wd_bf)


def _final_kernel(x_ref, g_ref, o_ref):
    x = x_ref[...]
    o_ref[...] = (x * lax.rsqrt(jnp.mean(x * x, axis=-1, keepdims=True) + EPS)) * g_ref[...]


def _final_norm(x, g_row, tm):
    t_tot, d = x.shape
    tok = pl.BlockSpec((tm, d), lambda t: (t, 0))
    return pl.pallas_call(
        _final_kernel,
        grid=(t_tot // tm,),
        in_specs=[tok, pl.BlockSpec((1, d), lambda t: (0, 0))],
        out_specs=tok,
        out_shape=jax.ShapeDtypeStruct((t_tot, d), F32),
        compiler_params=_cparams("arbitrary"),
        name="final_norm",
    )(x, g_row)


def kernel(x_prompt, x_sample, c_prompt, c_sample, state_pool, cache_k, cache_v, page_table, ada_w, ada_b, norm_mix, norm_ffn, norm_final, pool_w, pool_scale, attn_w_qkv, attn_w_o, lambda_q1, lambda_k1, lambda_q2, lambda_k2, subln, router_w, router_bias, moe_w_gate, moe_w_up, moe_w_down):
    bp, n_p, d = x_prompt.shape
    bs, n_s, _ = x_sample.shape
    depth = ada_w.shape[0]
    past_len = page_table.shape[1] * PAGE_SIZE
    tm_p, tm_s = 512, 256

    mods_s, mods_p = _ada_mods(c_sample, c_prompt, ada_w, ada_b)
    grp_p = _Group(mods_p.reshape(depth, bp, 1, N_MODS * d), per_token=False, seq=n_p)
    grp_s = _Group(jnp.repeat(mods_s, n_s, axis=1), per_token=True, seq=n_s)

    xp = x_prompt.reshape(bp * n_p, d)
    xs = x_sample.reshape(bs * n_s, d)

    norm_mix3 = norm_mix.reshape(depth, 1, d)
    norm_ffn3 = norm_ffn.reshape(depth, 1, d)
    pool_w_bf = pool_w.astype(BF16)
    pool_scale3 = pool_scale.reshape(-1, 1, d)
    wqkv_bf = attn_w_qkv.astype(BF16)
    wo_bf = attn_w_o.astype(BF16)
    router_wt = router_w.T
    router_b = router_bias.reshape(N_EXPERTS, 1)
    wg_bf = moe_w_gate.astype(BF16)
    wu_bf = moe_w_up.astype(BF16)
    wd_bf = moe_w_down.astype(BF16)

    cos_p, sin_p = _rope_tables(jnp.arange(n_p))
    cos_s, sin_s = _rope_tables(past_len + jnp.arange(n_s))
    cos_s = jnp.tile(cos_s, (tm_s // n_s, 1))
    sin_s = jnp.tile(sin_s, (tm_s // n_s, 1))

    pool_p, pool_s, kp_rows, vp_rows, ks_rows, vs_rows = [], [], [], [], [], []
    for i in range(depth):
        j = i // 2
        if i % 2 == 0:
            xp, st_p = _pool_prompt(xp, grp_p, i, norm_mix3, pool_w_bf, pool_scale3, j, bp, ts=tm_p)
            state_pad = jnp.pad(state_pool[j], ((0, 0), (1, 0), (0, 0)))
            xs, st_s = _pool_sample(xs, state_pad, grp_s, i, norm_mix3, pool_w_bf, pool_scale3, j, n_s)
            pool_p.append(st_p[:, 1:])
            pool_s.append(st_s[:, 1:])
        else:
            lam_init = 0.8 - 0.6 * math.exp(-0.3 * i)
            lam = (jnp.exp(jnp.sum(lambda_q1[j] * lambda_k1[j])) - jnp.exp(jnp.sum(lambda_q2[j] * lambda_k2[j]))
                   + lam_init).reshape(1).astype(F32)
            sub_row = subln[j].reshape(1, VAL_DIM)
            out_scale = 1.0 - lam_init

            q, kb, vb, kf, vf = _qkv(xp, grp_p, i, norm_mix3, wqkv_bf[j], cos_p, sin_p, tm_p)
            o = _attn_prompt(q, kb, vb, lam, sub_row, bp, n_p, out_scale)
            xp = _attn_out(xp, o, grp_p, i, wo_bf[j], tm_p)
            kp_rows.append(kf.reshape(bp, n_p, N_HEADS, VAL_DIM))
            vp_rows.append(vf.reshape(bp, n_p, N_HEADS, VAL_DIM))

            q, kb, vb, kf, vf = _qkv(xs, grp_s, i, norm_mix3, wqkv_bf[j], cos_s, sin_s, tm_s)
            r = n_s * N_HEADS
            o3 = _attn_sample(q.reshape(bs, r, VAL_DIM), kb.reshape(bs, r, VAL_DIM), vb.reshape(bs, r, VAL_DIM),
                              cache_k, cache_v, page_table, lam, sub_row, j, out_scale)
            xs = _attn_out(xs, o3.reshape(bs * n_s, d), grp_s, i, wo_bf[j], tm_s)
            ks_rows.append(kf.reshape(bs, n_s, N_HEADS, VAL_DIM))
            vs_rows.append(vf.reshape(bs, n_s, N_HEADS, VAL_DIM))

        xp = _moe_dense(xp, grp_p, i, norm_ffn3, router_wt, router_b, wg_bf, wu_bf, wd_bf, tm_p)
        xs = _moe_dense(xs, grp_s, i, norm_ffn3, router_wt, router_b, wg_bf, wu_bf, wd_bf, tm_s)

    g_row = norm_final.reshape(1, d)
    y_prompt = _final_norm(xp, g_row, tm_p).reshape(bp, n_p, d)
    y_sample = _final_norm(xs, g_row, tm_s).reshape(bs, n_s, d)
    return (y_prompt, y_sample, jnp.stack(pool_p), jnp.stack(pool_s),
            jnp.stack(kp_rows), jnp.stack(vp_rows), jnp.stack(ks_rows), jnp.stack(vs_rows))
```

```python
import functools
import math

import numpy as np

import jax
import jax.numpy as jnp
from jax import lax
from jax.experimental import pallas as pl
from jax.experimental.pallas import tpu as pltpu

F32 = jnp.float32
BF16 = jnp.bfloat16

D_MODEL = 1024
N_HEADS = 8
HEAD_DIM = 64
VAL_DIM = 128
ROPE_THETA = 10000.0
POOL_WINDOWS = (2, 4, 8, 16)
GROUP_CH = 256
MAX_WIN = 16
N_EXPERTS = 16
EXPERTS_PER_GROUP = 4
N_EXPERT_GROUPS = 4
D_EXPERT = 256
PAGE_SIZE = 128
EPS = 1e-6
N_MODS = 6
SUBLANES = 8
LANES = 128

_PAIRS = [(a, b) for a in range(EXPERTS_PER_GROUP) for b in range(a + 1, EXPERTS_PER_GROUP)]
N_PAIRS = len(_PAIRS)
N_CLASSES = N_EXPERT_GROUPS * N_PAIRS
_CLASS_E_LO = np.array([g * EXPERTS_PER_GROUP + a for g in range(N_EXPERT_GROUPS) for a, _ in _PAIRS], np.int32)
_CLASS_E_HI = np.array([g * EXPERTS_PER_GROUP + b for g in range(N_EXPERT_GROUPS) for _, b in _PAIRS], np.int32)

VMEM_LIMIT_BYTES = 52 * 1024 * 1024


def _cparams(*sem):
    return pltpu.CompilerParams(dimension_semantics=sem, vmem_limit_bytes=VMEM_LIMIT_BYTES)


def _normmod(x, g, sc, sh):
    r = lax.rsqrt(jnp.mean(x * x, axis=-1, keepdims=True) + EPS)
    return (x * r) * g * (1.0 + sc) + sh


def _dot(a, b):
    return jnp.dot(a, b, preferred_element_type=F32)


def _dot_nt(a, b):
    return lax.dot_general(a, b, (((1,), (1,)), ((), ())), preferred_element_type=F32)


def _ada_kernel(c_ref, w_ref, b_ref, os_ref, op_ref, *, n_s):
    c = c_ref[...]
    a = (c * jax.nn.sigmoid(c)).astype(BF16)
    r = _dot(a, w_ref[...].astype(BF16)) + b_ref[...]
    os_ref[...] = r[:n_s]
    op_ref[...] = r[n_s:]


def _ada_mods(c_sample, c_prompt, ada_w, ada_b, tn=768):
    depth, d, n = ada_w.shape
    n_s, n_p = c_sample.shape[0], c_prompt.shape[0]
    c_all = jnp.concatenate([c_sample, c_prompt], axis=0)
    return pl.pallas_call(
        functools.partial(_ada_kernel, n_s=n_s),
        grid=(depth, n // tn),
        in_specs=[
            pl.BlockSpec((n_s + n_p, d), lambda i, j: (0, 0)),
            pl.BlockSpec((None, d, tn), lambda i, j: (i, 0, j)),
            pl.BlockSpec((None, 1, tn), lambda i, j: (i, 0, j)),
        ],
        out_specs=[
            pl.BlockSpec((None, n_s, tn), lambda i, j: (i, 0, j)),
            pl.BlockSpec((None, n_p, tn), lambda i, j: (i, 0, j)),
        ],
        out_shape=[jax.ShapeDtypeStruct((depth, n_s, n), F32),
                   jax.ShapeDtypeStruct((depth, n_p, n), F32)],
        compiler_params=_cparams("arbitrary", "arbitrary"),
        name="ada_mods",
    )(c_all, ada_w, ada_b.reshape(depth, 1, n))


class _Group:
    def __init__(self, mods, per_token, seq):
        self.mods = mods
        self.per_token = per_token
        self.seq = seq

    def mod_spec(self, layer, chunk, tm):
        if self.per_token:
            return pl.BlockSpec((None, tm, D_MODEL), lambda t, *_: (layer, t, chunk))
        tpb = self.seq // tm
        return pl.BlockSpec((None, None, 1, D_MODEL), lambda t, *_: (layer, t // tpb, 0, chunk))


def _row_spec(layer, width=D_MODEL):
    return pl.BlockSpec((None, 1, width), lambda t, *_: (layer, 0, 0))


def _tok_spec(tm):
    return pl.BlockSpec((tm, D_MODEL), lambda t, *_: (t, 0))


def _stream_inputs(x, pend, grp, tm):
    specs, args = [_tok_spec(tm)], [x]
    if pend is not None:
        y, layer = pend
        specs += [_tile_rows_spec(tm), grp.mod_spec(layer, 5, tm)]
        args += [y, grp.mods]
    return specs, args


def _take_stream(refs, has_pend):
    if has_pend:
        return refs[0], refs[1], refs[2], refs[3:]
    return refs[0], None, None, refs[1:]


def _tile_rows_spec(tm):
    return pl.BlockSpec((tm * SUBLANES, LANES), lambda t, *_: (t, 0))


def _load_tile_rows(ref, r0, n, lane_groups):
    parts = [ref[pl.ds(r0 * SUBLANES + j, n, stride=SUBLANES), :] for j in lane_groups]
    return parts[0] if len(parts) == 1 else jnp.concatenate(parts, axis=1)


def _store_tile_rows(ref, value):
    n = value.shape[0]
    for j in range(D_MODEL // LANES):
        ref[pl.ds(j, n, stride=SUBLANES), :] = value[:, j * LANES:(j + 1) * LANES]


def _load_stream(x_ref, y_ref, g2_ref, rows=None, cols=None):
    r0, r1 = rows if rows is not None else (0, x_ref.shape[0])
    c0, c1 = cols if cols is not None else (0, x_ref.shape[1])
    x = x_ref[r0:r1, c0:c1]
    if y_ref is None:
        return x
    g2 = g2_ref[:, c0:c1] if g2_ref.shape[0] == 1 else g2_ref[r0:r1, c0:c1]
    return x + g2 * _load_tile_rows(y_ref, r0, r1 - r0, range(c0 // LANES, c1 // LANES))


def _pool_p_kernel(*refs, ts, tpb, rc, has_pend):
    x_ref, y_ref, g2_ref, refs = _take_stream(refs, has_pend)
    if has_pend:
        hx_ref, hy_ref, refs = refs[0], refs[1], refs[2:]
    else:
        hx_ref, hy_ref, refs = refs[0], None, refs[1:]
    sh_ref, sc_ref, gt_ref, nrm_ref, w_ref, ps_ref, xo_ref, st_ref, ext_ref = refs
    s = pl.program_id(0) % tpb
    g = nrm_ref[...]
    sc = sc_ref[...]
    sh = sh_ref[...]
    hh = _normmod(_load_stream(hx_ref, hy_ref, g2_ref), g, sc, sh)
    ext_ref[0:MAX_WIN, :] = jnp.where(s == 0, 0.0, hh)
    ext_ref[MAX_WIN:, :] = _normmod(_load_stream(x_ref, y_ref, g2_ref), g, sc, sh)
    for r0 in range(0, ts, rc):
        pos = s * ts + r0 + lax.broadcasted_iota(jnp.int32, (rc, 1), 0)
        for gi, w in enumerate(POOL_WINDOWS):
            c0, c1 = gi * GROUP_CH, (gi + 1) * GROUP_CH
            cs = slice(c0, c1)
            base = MAX_WIN + r0
            h = ext_ref[base:base + rc, cs]
            acc = h
            for k in range(1, w):
                acc = acc + ext_ref[base - k:base - k + rc, cs]
            cnt = jnp.minimum(w, pos + 1).astype(F32)
            pooled = acc / cnt - h
            o = _dot(pooled.astype(BF16), w_ref[gi]) * ps_ref[:, cs]
            xo_ref[r0:r0 + rc, cs] = (_load_stream(x_ref, y_ref, g2_ref, (r0, r0 + rc), (c0, c1))
                                      + gt_ref[:, cs] * o)

    @pl.when(s == tpb - 1)
    def _():
        st_ref[...] = ext_ref[ts:ts + MAX_WIN, :]


def _pool_prompt(x, pend, grp, layer, norm_mix, pool_w_bf, pool_scale, j, n_batch, ts=512, rc=128):
    t_tot, d = x.shape
    tpb = grp.seq // ts
    hb = ts // MAX_WIN
    halo = pl.BlockSpec((MAX_WIN, d), lambda t: (jnp.maximum(t * hb - 1, 0), 0))
    halo_y = pl.BlockSpec((MAX_WIN * SUBLANES, LANES), lambda t: (jnp.maximum(t * hb - 1, 0), 0))
    s_specs, s_args = _stream_inputs(x, pend, grp, ts)
    h_specs, h_args = ([halo, halo_y], [x, pend[0]]) if pend is not None else ([halo], [x])
    return pl.pallas_call(
        functools.partial(_pool_p_kernel, ts=ts, tpb=tpb, rc=rc, has_pend=pend is not None),
        grid=(t_tot // ts,),
        in_specs=s_specs + h_specs + [
            grp.mod_spec(layer, 0, ts), grp.mod_spec(layer, 1, ts), grp.mod_spec(layer, 2, ts),
            _row_spec(layer),
            pl.BlockSpec((None, 4, GROUP_CH, GROUP_CH), lambda t: (j, 0, 0, 0)),
            _row_spec(j),
        ],
        out_specs=[
            pl.BlockSpec((ts, d), lambda t: (t, 0)),
            pl.BlockSpec((None, MAX_WIN, d), lambda t: (t // tpb, 0, 0)),
        ],
        out_shape=[jax.ShapeDtypeStruct((t_tot, d), F32),
                   jax.ShapeDtypeStruct((n_batch, MAX_WIN, d), F32)],
        scratch_shapes=[pltpu.VMEM((ts + MAX_WIN, d), F32)],
        compiler_params=_cparams("arbitrary"),
        name="pool_prompt",
    )(*s_args, *h_args, grp.mods, grp.mods, grp.mods, norm_mix, pool_w_bf, pool_scale)


def _pool_s_kernel(*refs, bb, n_new, has_pend):
    x_ref, y_ref, g2_ref, refs = _take_stream(refs, has_pend)
    st_ref, sh_ref, sc_ref, gt_ref, nrm_ref, w_ref, ps_ref, xo_ref, so_ref, ext_ref = refs
    x = _load_stream(x_ref, y_ref, g2_ref)
    h2 = _normmod(x, nrm_ref[...], sc_ref[...], sh_ref[...])
    ext_ref[:, 0:MAX_WIN, :] = st_ref[...]
    ext_ref[:, MAX_WIN:, :] = h2.reshape(bb, n_new, D_MODEL)
    outs = []
    for gi, w in enumerate(POOL_WINDOWS):
        cs = slice(gi * GROUP_CH, (gi + 1) * GROUP_CH)
        h = ext_ref[:, MAX_WIN:MAX_WIN + n_new, cs]
        acc = h
        for k in range(1, w):
            acc = acc + ext_ref[:, MAX_WIN - k:MAX_WIN - k + n_new, cs]
        pooled = (acc / float(w) - h).reshape(bb * n_new, GROUP_CH)
        outs.append(_dot(pooled.astype(BF16), w_ref[gi]))
    out = jnp.concatenate(outs, axis=-1) * ps_ref[...]
    xo_ref[...] = x + gt_ref[...] * out
    so_ref[...] = ext_ref[:, n_new:n_new + MAX_WIN, :]


def _pool_sample(x, pend, state_pad, grp, layer, norm_mix, pool_w_bf, pool_scale, j, n_new, bb=32):
    t_tot, d = x.shape
    n_seq = t_tot // n_new
    tm = bb * n_new
    s_specs, s_args = _stream_inputs(x, pend, grp, tm)
    return pl.pallas_call(
        functools.partial(_pool_s_kernel, bb=bb, n_new=n_new, has_pend=pend is not None),
        grid=(n_seq // bb,),
        in_specs=s_specs + [
            pl.BlockSpec((bb, MAX_WIN, d), lambda t: (t, 0, 0)),
            grp.mod_spec(layer, 0, tm), grp.mod_spec(layer, 1, tm), grp.mod_spec(layer, 2, tm),
            _row_spec(layer),
            pl.BlockSpec((None, 4, GROUP_CH, GROUP_CH), lambda t: (j, 0, 0, 0)),
            _row_spec(j),
        ],
        out_specs=[
            pl.BlockSpec((tm, d), lambda t: (t, 0)),
            pl.BlockSpec((bb, MAX_WIN, d), lambda t: (t, 0, 0)),
        ],
        out_shape=[jax.ShapeDtypeStruct((t_tot, d), F32),
                   jax.ShapeDtypeStruct((n_seq, MAX_WIN, d), F32)],
        scratch_shapes=[pltpu.VMEM((bb, MAX_WIN + n_new, d), F32)],
        compiler_params=_cparams("arbitrary"),
        name="pool_sample",
    )(*s_args, state_pad, grp.mods, grp.mods, grp.mods, norm_mix, pool_w_bf, pool_scale)


def _qkv_kernel(*refs, has_pend):
    x_ref, y_ref, g2_ref, refs = _take_stream(refs, has_pend)
    sh_ref, sc_ref, nrm_ref, w_ref, cos_ref, sin_ref, q_ref, kb_ref, vb_ref, kf_ref, vf_ref = refs
    h = _normmod(_load_stream(x_ref, y_ref, g2_ref), nrm_ref[...], sc_ref[...], sh_ref[...]).astype(BF16)
    cos = cos_ref[...]
    sin = sin_ref[...]
    lane = lax.broadcasted_iota(jnp.int32, (1, VAL_DIM), 1)
    first_half = (lane % HEAD_DIM) < (HEAD_DIM // 2)

    def rope(xh):
        rot = jnp.where(first_half, pltpu.roll(xh, VAL_DIM - HEAD_DIM // 2, 1),
                        pltpu.roll(xh, HEAD_DIM // 2, 1))
        return xh * cos + rot * sin

    pw = 2 * VAL_DIM
    for c0 in range(0, D_MODEL, pw):
        q2 = _dot(h, w_ref[:, c0:c0 + pw])
        k2 = _dot(h, w_ref[:, D_MODEL + c0:D_MODEL + c0 + pw])
        v2 = _dot(h, w_ref[:, 2 * D_MODEL + c0:2 * D_MODEL + c0 + pw])
        for u in range(2):
            cs = slice(c0 + u * VAL_DIM, c0 + (u + 1) * VAL_DIM)
            us = slice(u * VAL_DIM, (u + 1) * VAL_DIM)
            q_ref[:, cs] = (rope(q2[:, us]) * (HEAD_DIM ** -0.5)).astype(BF16)
            kr = rope(k2[:, us])
            kf_ref[:, cs] = kr
            kb_ref[:, cs] = kr.astype(BF16)
        vf_ref[:, c0:c0 + pw] = v2
        vb_ref[:, c0:c0 + pw] = v2.astype(BF16)


def _qkv(x, pend, grp, layer, norm_mix, wqkv_bf, cos_tab, sin_tab, tm):
    t_tot, d = x.shape
    n_pos_tiles = cos_tab.shape[0] // tm
    tok = _tok_spec(tm)
    tab = pl.BlockSpec((tm, VAL_DIM), lambda t: (t % n_pos_tiles, 0))
    s_specs, s_args = _stream_inputs(x, pend, grp, tm)
    return pl.pallas_call(
        functools.partial(_qkv_kernel, has_pend=pend is not None),
        grid=(t_tot // tm,),
        in_specs=s_specs + [grp.mod_spec(layer, 0, tm), grp.mod_spec(layer, 1, tm), _row_spec(layer),
                            pl.BlockSpec((d, 3 * d), lambda t: (0, 0)), tab, tab],
        out_specs=[tok] * 5,
        out_shape=[jax.ShapeDtypeStruct((t_tot, d), BF16)] * 3 + [jax.ShapeDtypeStruct((t_tot, d), F32)] * 2,
        compiler_params=_cparams("arbitrary"),
        name="qkv_rope",
    )(*s_args, grp.mods, grp.mods, norm_mix, wqkv_bf, cos_tab, sin_tab)


def _rope_tables(pos):
    inv = 1.0 / (ROPE_THETA ** (jnp.arange(0, HEAD_DIM, 2, dtype=F32) / HEAD_DIM))
    ang = pos.astype(F32)[:, None] * inv[None, :]
    cos, sin = jnp.cos(ang), jnp.sin(ang)
    cos_t = jnp.concatenate([cos, cos, cos, cos], axis=-1)
    sin_t = jnp.concatenate([-sin, sin, -sin, sin], axis=-1)
    return cos_t, sin_t


def _split_components(q):
    lane = lax.broadcasted_iota(jnp.int32, (1, VAL_DIM), 1)
    zero = jnp.zeros_like(q)
    return jnp.concatenate([jnp.where(lane < HEAD_DIM, q, zero), jnp.where(lane >= HEAD_DIM, q, zero)], axis=0)


def _with_ones(v):
    return jnp.concatenate([v, jnp.ones_like(v)], axis=1)


def _online_update(s, m_old, acc_old, v1):
    chunks = [s[:, u:u + VAL_DIM] for u in range(0, s.shape[1], VAL_DIM)]
    cm = chunks[0]
    for c in chunks[1:]:
        cm = jnp.maximum(cm, c)
    m_new = jnp.maximum(m_old, jnp.max(cm, axis=-1, keepdims=True))
    alpha = jnp.exp(m_old - m_new)
    p = jnp.concatenate([jnp.exp(c - m_new) for c in chunks], axis=1).astype(BF16)
    acc_new = jnp.concatenate([alpha, alpha], axis=1) * acc_old + _dot(p, v1)
    return m_new, acc_new


def _diff_finish(acc, lam, subln, out_scale, r):
    o = acc[:, :VAL_DIM] / acc[:, VAL_DIM:]
    od = o[:r] - lam * o[r:]
    y = od * lax.rsqrt(jnp.mean(od * od, axis=-1, keepdims=True) + EPS)
    return y * subln * out_scale


def _attn_p_kernel(lam_ref, q_ref, k_ref, v_ref, sub_ref, o_ref, qs_ref, m_ref, acc_ref, *, tq, rc, out_scale):
    qi = pl.program_id(2)
    r = 2 * tq
    qs_ref[...] = _split_components(q_ref[...])
    m_ref[...] = jnp.full(m_ref.shape, -jnp.inf, F32)
    acc_ref[...] = jnp.zeros(acc_ref.shape, F32)

    def step(kj, masked):
        off = pl.multiple_of(kj * tq, tq)
        k = k_ref[pl.ds(off, tq), :]
        v1 = _with_ones(v_ref[pl.ds(off, tq), :])
        for r0 in range(0, r, rc):
            rows = slice(r0, r0 + rc)
            nk = (r0 % tq) + rc if masked else tq
            s = _dot_nt(qs_ref[rows, :], k[:nk])
            if masked:
                row = (r0 + lax.broadcasted_iota(jnp.int32, (rc, 1), 0)) % tq
                col = lax.broadcasted_iota(jnp.int32, (1, nk), 1)
                s = jnp.where(col <= row, s, -jnp.inf)
            m_new, acc_new = _online_update(s, m_ref[rows, :], acc_ref[rows, :], v1[:nk])
            m_ref[rows, :] = m_new
            acc_ref[rows, :] = acc_new

    def body(t, carry):
        step(2 * t, False)
        step(2 * t + 1, False)
        return carry

    lax.fori_loop(0, qi // 2, body, 0)

    @pl.when(qi % 2 == 1)
    def _():
        step(qi - 1, False)

    step(qi, True)
    o_ref[...] = _diff_finish(acc_ref[...], lam_ref[0], sub_ref[...], out_scale, tq).astype(BF16)


def _attn_prompt(q, k, v, lam, subln_row, n_batch, seq, out_scale, tq=512, rc=256):
    t_tot, d = q.shape
    nq = seq // tq
    return pl.pallas_call(
        functools.partial(_attn_p_kernel, tq=tq, rc=rc, out_scale=out_scale),
        grid=(n_batch, N_HEADS, nq),
        in_specs=[
            pl.BlockSpec(memory_space=pltpu.SMEM),
            pl.BlockSpec((tq, VAL_DIM), lambda b, h, i: (b * nq + i, h)),
            pl.BlockSpec((seq, VAL_DIM), lambda b, h, i: (b, h)),
            pl.BlockSpec((seq, VAL_DIM), lambda b, h, i: (b, h)),
            pl.BlockSpec((1, VAL_DIM), lambda b, h, i: (0, 0)),
        ],
        out_specs=pl.BlockSpec((tq, VAL_DIM), lambda b, h, i: (b * nq + i, h)),
        out_shape=jax.ShapeDtypeStruct((t_tot, d), BF16),
        scratch_shapes=[pltpu.VMEM((2 * tq, VAL_DIM), BF16), pltpu.VMEM((2 * tq, VAL_DIM), F32),
                        pltpu.VMEM((2 * tq, 2 * VAL_DIM), F32)],
        compiler_params=_cparams("arbitrary", "arbitrary", "arbitrary"),
        name="attn_prompt",
    )(lam, q, k, v, subln_row)


def _attn_s_kernel(pt_ref, lam_ref, q_ref, kn_ref, vn_ref, sub_ref, *rest, n_pages, n_new, out_scale):
    del pt_ref
    k_refs = rest[:n_pages]
    v_refs = rest[n_pages:2 * n_pages]
    o_ref = rest[2 * n_pages]
    r = n_new * N_HEADS
    qs = _split_components(q_ref[...])
    row = lax.broadcasted_iota(jnp.int32, (2 * r, 1), 0)
    row_h = row % N_HEADS
    row_t = (row // N_HEADS) % n_new
    kcols = PAGE_SIZE * N_HEADS
    col_h = lax.broadcasted_iota(jnp.int32, (1, kcols), 1) % N_HEADS
    same_head = row_h == col_h

    m = jnp.full((2 * r, VAL_DIM), -jnp.inf, F32)
    acc = jnp.zeros((2 * r, 2 * VAL_DIM), F32)
    for p_i in range(n_pages):
        k = k_refs[p_i][...].reshape(kcols, VAL_DIM).astype(BF16)
        v = v_refs[p_i][...].reshape(kcols, VAL_DIM).astype(BF16)
        s = jnp.where(same_head, _dot_nt(qs, k), -jnp.inf)
        m, acc = _online_update(s, m, acc, _with_ones(v))

    ncol = lax.broadcasted_iota(jnp.int32, (1, VAL_DIM), 1)
    new_ok = (ncol < r) & (row_h == ncol % N_HEADS) & (ncol // N_HEADS <= row_t)
    pad = jnp.zeros((VAL_DIM - r, VAL_DIM), BF16)
    kn = jnp.concatenate([kn_ref[...], pad], axis=0)
    vn = jnp.concatenate([vn_ref[...], pad], axis=0)
    s = jnp.where(new_ok, _dot_nt(qs, kn), -jnp.inf)
    m, acc = _online_update(s, m, acc, _with_ones(vn))
    o_ref[...] = _diff_finish(acc, lam_ref[0], sub_ref[...], out_scale, r).astype(BF16)


def _attn_sample(q3, kn3, vn3, cache_k, cache_v, page_table, lam, subln_row, j, out_scale):
    n_seq, r, _ = q3.shape
    n_pages = page_table.shape[1]
    n_new = r // N_HEADS
    row3 = pl.BlockSpec((None, r, VAL_DIM), lambda b, pt: (b, 0, 0))

    def page_spec(p):
        return pl.BlockSpec((None, None, PAGE_SIZE, N_HEADS, VAL_DIM), lambda b, pt: (j, pt[b, p], 0, 0, 0))

    grid_spec = pltpu.PrefetchScalarGridSpec(
        num_scalar_prefetch=1,
        grid=(n_seq,),
        in_specs=[pl.BlockSpec(memory_space=pltpu.SMEM), row3, row3, row3,
                  pl.BlockSpec((1, VAL_DIM), lambda b, pt: (0, 0))]
        + [page_spec(p) for p in range(n_pages)] * 2,
        out_specs=row3,
    )
    return pl.pallas_call(
        functools.partial(_attn_s_kernel, n_pages=n_pages, n_new=n_new, out_scale=out_scale),
        grid_spec=grid_spec,
        out_shape=jax.ShapeDtypeStruct((n_seq, r, VAL_DIM), BF16),
        compiler_params=_cparams("arbitrary"),
        name="attn_sample",
    )(page_table, lam, q3, kn3, vn3, subln_row, *([cache_k] * n_pages), *([cache_v] * n_pages))


def _wo_kernel(*refs, has_pend):
    x_ref, y_ref, g2_ref, refs = _take_stream(refs, has_pend)
    o_ref, gt_ref, w_ref, xo_ref = refs
    xo_ref[...] = _load_stream(x_ref, y_ref, g2_ref) + gt_ref[...] * _dot(o_ref[...], w_ref[...])


def _attn_out(x, pend, o, grp, layer, wo_bf, tm):
    t_tot, d = x.shape
    tok = _tok_spec(tm)
    s_specs, s_args = _stream_inputs(x, pend, grp, tm)
    return pl.pallas_call(
        functools.partial(_wo_kernel, has_pend=pend is not None),
        grid=(t_tot // tm,),
        in_specs=s_specs + [tok, grp.mod_spec(layer, 2, tm), pl.BlockSpec((d, d), lambda t: (0, 0))],
        out_specs=tok,
        out_shape=jax.ShapeDtypeStruct((t_tot, d), F32),
        compiler_params=_cparams("arbitrary"),
        name="attn_out",
    )(*s_args, o, grp.mods, wo_bf)


def _route(logits_t, bias):
    s_all = jax.nn.sigmoid(logits_t)
    sb_all = s_all + bias
    s = [s_all[e:e + 1, :] for e in range(N_EXPERTS)]
    sb = [sb_all[e:e + 1, :] for e in range(N_EXPERTS)]
    n = EXPERTS_PER_GROUP
    gscore = []
    for g in range(N_EXPERT_GROUPS):
        v = sb[g * n:(g + 1) * n]
        best = v[0] + v[1]
        for a in range(n):
            for b in range(a + 1, n):
                if (a, b) != (0, 1):
                    best = jnp.maximum(best, v[a] + v[b])
        gscore.append(best)
    gsel = jnp.zeros(gscore[0].shape, jnp.int32)
    gbest = gscore[0]
    for g in range(1, N_EXPERT_GROUPS):
        better = gscore[g] > gbest
        gsel = jnp.where(better, g, gsel)
        gbest = jnp.where(better, gscore[g], gbest)

    def pick(rows, r):
        out = rows[r]
        for g in range(1, N_EXPERT_GROUPS):
            out = jnp.where(gsel == g, rows[g * n + r], out)
        return out

    vb = [pick(sb, r) for r in range(n)]
    vu = [pick(s, r) for r in range(n)]

    def argmax_first(vals):
        idx = jnp.zeros(gsel.shape, jnp.int32)
        best = vals[0]
        for r in range(1, n):
            better = vals[r] > best
            idx = jnp.where(better, r, idx)
            best = jnp.where(better, vals[r], best)
        return idx

    i1 = argmax_first(vb)
    i2 = argmax_first([jnp.where(i1 == r, -jnp.inf, vb[r]) for r in range(n)])

    def take(vals, idx):
        out = vals[0]
        for r in range(1, n):
            out = jnp.where(idx == r, vals[r], out)
        return out

    w1 = take(vu, i1)
    w2 = take(vu, i2)
    tot = w1 + w2
    a = jnp.minimum(i1, i2)
    b = jnp.maximum(i1, i2)
    pair = jnp.where(a == 0, 0, jnp.where(a == 1, 3, 5)) + (b - a - 1)
    cls = gsel * N_PAIRS + pair
    first_is_lo = i1 < i2
    g_lo = jnp.where(first_is_lo, w1, w2) / tot
    g_hi = jnp.where(first_is_lo, w2, w1) / tot
    return cls, g_lo, g_hi


def _router_logits(h2, h2_bf, rw):
    rw_hi = rw.astype(BF16)
    rw_lo = (rw - rw_hi.astype(F32)).astype(BF16)
    h_lo = (h2 - h2_bf.astype(F32)).astype(BF16)
    return _dot_nt(rw_hi, h2_bf) + (_dot_nt(rw_hi, h_lo) + _dot_nt(rw_lo, h2_bf))


def _moe_route_kernel(*refs, tm, has_pend):
    x_ref, y_ref, g2_ref, refs = _take_stream(refs, has_pend)
    sh_ref, sc_ref, nrm_ref, rw_ref, rb_ref, h_ref, info_ref = refs
    h2 = _normmod(_load_stream(x_ref, y_ref, g2_ref), nrm_ref[...], sc_ref[...], sh_ref[...])
    h2_bf = h2.astype(BF16)
    _store_tile_rows(h_ref, h2)
    cls, g_lo, g_hi = _route(_router_logits(h2, h2_bf, rw_ref[...]), rb_ref[...])
    row = lax.broadcasted_iota(jnp.int32, (SUBLANES, tm), 0)
    info_ref[...] = jnp.where(row == 0, cls.astype(F32), jnp.where(row == 1, g_lo, jnp.where(row == 2, g_hi, 0.0)))


def _moe_route(x, pend, grp, layer, norm_ffn, router_wt, router_b, tm):
    t_tot, d = x.shape
    s_specs, s_args = _stream_inputs(x, pend, grp, tm)
    return pl.pallas_call(
        functools.partial(_moe_route_kernel, tm=tm, has_pend=pend is not None),
        grid=(t_tot // tm,),
        in_specs=s_specs + [grp.mod_spec(layer, 3, tm), grp.mod_spec(layer, 4, tm), _row_spec(layer),
                            pl.BlockSpec((N_EXPERTS, d), lambda t: (0, 0)),
                            pl.BlockSpec((N_EXPERTS, 1), lambda t: (0, 0))],
        out_specs=[_tile_rows_spec(tm), pl.BlockSpec((SUBLANES, tm), lambda t: (0, t))],
        out_shape=[jax.ShapeDtypeStruct((t_tot * SUBLANES, LANES), F32),
                   jax.ShapeDtypeStruct((SUBLANES, t_tot), F32)],
        compiler_params=_cparams("arbitrary"),
        name="moe_route",
    )(*s_args, grp.mods, grp.mods, norm_ffn, router_wt, router_b)


def _moe_plan(info, tm):
    t = info.shape[1]
    n_tiles = t // tm + N_CLASSES
    cls = info[0].astype(jnp.int32)
    onehot = (cls[:, None] == jnp.arange(N_CLASSES, dtype=jnp.int32)[None, :]).astype(jnp.int32)
    csum = jnp.cumsum(onehot, axis=0)
    counts = csum[-1]
    rank = jnp.sum((csum - onehot) * onehot, axis=1)
    tiles = (counts + tm - 1) // tm
    tile_end = jnp.cumsum(tiles)
    row_start = (tile_end - tiles) * tm
    n_used = tile_end[-1]
    pos = jnp.sum(onehot * row_start[None, :], axis=1) + rank
    tok = jnp.arange(t, dtype=jnp.int32)
    src = jnp.zeros((n_tiles * tm,), jnp.int32).at[pos].set(tok, unique_indices=True)
    valid = jnp.zeros((n_tiles * tm,), jnp.int32).at[pos].set(1, unique_indices=True)
    nv = jnp.sum(valid.reshape(n_tiles, tm), axis=1)
    tile_id = jnp.arange(n_tiles, dtype=jnp.int32)
    tcls = jnp.sum((tile_id[:, None] >= tile_end[None, :]).astype(jnp.int32), axis=1)
    tcls = jnp.where(tile_id < n_used, tcls, tcls[jnp.maximum(n_used - 1, 0)])
    gates = jnp.where(valid[None, :] > 0, info[1:3][:, src], 0.0)
    gates = gates.reshape(2, n_tiles, tm).transpose(1, 0, 2)
    gates = jnp.concatenate([gates, jnp.zeros((n_tiles, SUBLANES - 2, tm), F32)], axis=1)
    e_lo = jnp.asarray(_CLASS_E_LO)[tcls]
    e_hi = jnp.asarray(_CLASS_E_HI)[tcls]
    return e_lo, e_hi, n_used.reshape(1).astype(jnp.int32), nv.astype(jnp.int32), src, gates


def _moe_expert_kernel(elo_ref, ehi_ref, nused_ref, nv_ref, src_ref,
                       h_hbm, g_ref, wg_lo, wg_hi, wu_lo, wu_hi, wd_lo, wd_hi,
                       y_hbm, hbuf, obuf, gsem, ssem, *, tm):
    del elo_ref, ehi_ref
    i = pl.program_id(0)
    n_used = nused_ref[0]
    slot = i % 2
    tile_rows = tm * SUBLANES

    def token_rows(tile, r):
        return pl.ds(pl.multiple_of(src_ref[tile * tm + r] * SUBLANES, SUBLANES), SUBLANES)

    def row_in(tile, sl, r):
        return pltpu.make_async_copy(h_hbm.at[token_rows(tile, r)], hbuf.at[sl, pl.ds(r * SUBLANES, SUBLANES)],
                                     gsem.at[sl])

    def row_out(tile, sl, r):
        return pltpu.make_async_copy(obuf.at[sl, pl.ds(pl.multiple_of(r * SUBLANES, SUBLANES), SUBLANES)],
                                     y_hbm.at[token_rows(tile, r)], ssem.at[sl])

    def start_gather(tile, sl):
        for r in range(tm):
            row_in(tile, sl, r).start()

    def wait_gather(sl):
        pltpu.make_async_copy(h_hbm.at[pl.ds(0, tile_rows)], hbuf.at[sl], gsem.at[sl]).wait()

    def start_scatter(tile, sl):
        n = nv_ref[tile]

        @pl.when(n == tm)
        def _():
            for r in range(tm):
                row_out(tile, sl, r).start()

        @pl.when(n < tm)
        def _():
            def body(r, c):
                row_out(tile, sl, r).start()
                return c
            lax.fori_loop(0, n, body, 0)

    def wait_scatter(tile, sl):
        n_rows = pl.multiple_of(nv_ref[tile] * SUBLANES, SUBLANES)
        pltpu.make_async_copy(obuf.at[sl, pl.ds(0, n_rows)], y_hbm.at[pl.ds(0, n_rows)], ssem.at[sl]).wait()

    @pl.when(i == 0)
    def _():
        start_gather(0, 0)

    @pl.when(i < n_used)
    def _():
        @pl.when(i + 1 < n_used)
        def _():
            start_gather(i + 1, 1 - slot)

        wait_gather(slot)

        @pl.when(i >= 2)
        def _():
            wait_scatter(i - 2, slot)

        h = _load_tile_rows(hbuf.at[slot], 0, tm, range(D_MODEL // LANES)).astype(BF16)
        pad = jnp.zeros((LANES - SUBLANES, tm), F32)
        gates = jnp.concatenate([g_ref[...], pad], axis=0).T

        def expert(wg, wu, gate):
            hg = _dot(h, wg[...])
            hu = _dot(h, wu[...])
            return ((hg * jax.nn.sigmoid(hg)) * hu * gate).astype(BF16)

        y = _dot(expert(wg_lo, wu_lo, gates[:, 0:1]), wd_lo[...]) + _dot(expert(wg_hi, wu_hi, gates[:, 1:2]), wd_hi[...])
        _store_tile_rows(obuf.at[slot], y)
        start_scatter(i, slot)

        @pl.when(i == n_used - 1)
        def _():
            wait_scatter(i, slot)

            @pl.when(i >= 1)
            def _():
                wait_scatter(i - 1, 1 - slot)


def _moe_experts(h2, plan, layer, wg_bf, wu_bf, wd_bf, tm):
    e_lo, e_hi, n_used, nv, src, gates = plan
    d = D_MODEL
    n_tiles = gates.shape[0]

    def w_spec(shape, which):
        return pl.BlockSpec((None, None) + shape, lambda i, elo, ehi, *_: (layer, (elo, ehi)[which][i], 0, 0))

    grid_spec = pltpu.PrefetchScalarGridSpec(
        num_scalar_prefetch=5,
        grid=(n_tiles,),
        in_specs=[pl.BlockSpec(memory_space=pl.ANY),
                  pl.BlockSpec((None, SUBLANES, tm), lambda i, *_: (i, 0, 0)),
                  w_spec((d, D_EXPERT), 0), w_spec((d, D_EXPERT), 1),
                  w_spec((d, D_EXPERT), 0), w_spec((d, D_EXPERT), 1),
                  w_spec((D_EXPERT, d), 0), w_spec((D_EXPERT, d), 1)],
        out_specs=pl.BlockSpec(memory_space=pl.ANY),
        scratch_shapes=[pltpu.VMEM((2, tm * SUBLANES, LANES), F32), pltpu.VMEM((2, tm * SUBLANES, LANES), F32),
                        pltpu.SemaphoreType.DMA((2,)), pltpu.SemaphoreType.DMA((2,))],
    )
    return pl.pallas_call(
        functools.partial(_moe_expert_kernel, tm=tm),
        grid_spec=grid_spec,
        out_shape=jax.ShapeDtypeStruct(h2.shape, F32),
        compiler_params=_cparams("arbitrary"),
        name="moe_experts",
    )(e_lo, e_hi, n_used, nv, src, h2, gates, wg_bf, wg_bf, wu_bf, wu_bf, wd_bf, wd_bf)


def _moe(x, pend, grp, layer, norm_ffn, router_wt, router_b, wg_bf, wu_bf, wd_bf, tm_route, tm_expert):
    h2, info = _moe_route(x, pend, grp, layer, norm_ffn, router_wt, router_b, tm_route)
    return _moe_experts(h2, _moe_plan(info, tm_expert), layer, wg_bf, wu_bf, wd_bf, tm_expert)


def _final_kernel(*refs, has_pend):
    x_ref, y_ref, g2_ref, refs = _take_stream(refs, has_pend)
    g_ref, o_ref = refs
    x = _load_stream(x_ref, y_ref, g2_ref)
    o_ref[...] = (x * lax.rsqrt(jnp.mean(x * x, axis=-1, keepdims=True) + EPS)) * g_ref[...]


def _final_norm(x, pend, grp, g_row, tm):
    t_tot, d = x.shape
    s_specs, s_args = _stream_inputs(x, pend, grp, tm)
    return pl.pallas_call(
        functools.partial(_final_kernel, has_pend=pend is not None),
        grid=(t_tot // tm,),
        in_specs=s_specs + [pl.BlockSpec((1, d), lambda t: (0, 0))],
        out_specs=_tok_spec(tm),
        out_shape=jax.ShapeDtypeStruct((t_tot, d), F32),
        compiler_params=_cparams("arbitrary"),
        name="final_norm",
    )(*s_args, g_row)


def kernel(x_prompt, x_sample, c_prompt, c_sample, state_pool, cache_k, cache_v, page_table, ada_w, ada_b, norm_mix, norm_ffn, norm_final, pool_w, pool_scale, attn_w_qkv, attn_w_o, lambda_q1, lambda_k1, lambda_q2, lambda_k2, subln, router_w, router_bias, moe_w_gate, moe_w_up, moe_w_down):
    bp, n_p, d = x_prompt.shape
    bs, n_s, _ = x_sample.shape
    depth = ada_w.shape[0]
    past_len = page_table.shape[1] * PAGE_SIZE
    tm_p, tm_s = 512, 256
    tme_p, tme_s = 256, 128

    mods_s, mods_p = _ada_mods(c_sample, c_prompt, ada_w, ada_b)
    grp_p = _Group(mods_p.reshape(depth, bp, 1, N_MODS * d), per_token=False, seq=n_p)
    grp_s = _Group(jnp.repeat(mods_s, n_s, axis=1), per_token=True, seq=n_s)

    xp = x_prompt.reshape(bp * n_p, d)
    xs = x_sample.reshape(bs * n_s, d)
    pend_p = pend_s = None

    norm_mix3 = norm_mix.reshape(depth, 1, d)
    norm_ffn3 = norm_ffn.reshape(depth, 1, d)
    pool_w_bf = pool_w.astype(BF16)
    pool_scale3 = pool_scale.reshape(-1, 1, d)
    wqkv_bf = attn_w_qkv.astype(BF16)
    wo_bf = attn_w_o.astype(BF16)
    router_wt = router_w.T
    router_b = router_bias.reshape(N_EXPERTS, 1)
    wg_bf = moe_w_gate.astype(BF16)
    wu_bf = moe_w_up.astype(BF16)
    wd_bf = moe_w_down.astype(BF16)

    cos_p, sin_p = _rope_tables(jnp.arange(n_p))
    cos_s, sin_s = _rope_tables(past_len + jnp.arange(n_s))
    cos_s = jnp.tile(cos_s, (tm_s // n_s, 1))
    sin_s = jnp.tile(sin_s, (tm_s // n_s, 1))

    pool_p, pool_s, kp_rows, vp_rows, ks_rows, vs_rows = [], [], [], [], [], []
    for i in range(depth):
        j = i // 2
        if i % 2 == 0:
            xp, st_p = _pool_prompt(xp, pend_p, grp_p, i, norm_mix3, pool_w_bf, pool_scale3, j, bp, ts=tm_p)
            state_pad = jnp.pad(state_pool[j], ((0, 0), (1, 0), (0, 0)))
            xs, st_s = _pool_sample(xs, pend_s, state_pad, grp_s, i, norm_mix3, pool_w_bf, pool_scale3, j, n_s)
            pool_p.append(st_p[:, 1:])
            pool_s.append(st_s[:, 1:])
        else:
            lam_init = 0.8 - 0.6 * math.exp(-0.3 * i)
            lam = (jnp.exp(jnp.sum(lambda_q1[j] * lambda_k1[j])) - jnp.exp(jnp.sum(lambda_q2[j] * lambda_k2[j]))
                   + lam_init).reshape(1).astype(F32)
            sub_row = subln[j].reshape(1, VAL_DIM)
            out_scale = 1.0 - lam_init

            q, kb, vb, kf, vf = _qkv(xp, pend_p, grp_p, i, norm_mix3, wqkv_bf[j], cos_p, sin_p, tm_p)
            o = _attn_prompt(q, kb, vb, lam, sub_row, bp, n_p, out_scale)
            xp = _attn_out(xp, pend_p, o, grp_p, i, wo_bf[j], tm_p)
            kp_rows.append(kf.reshape(bp, n_p, N_HEADS, VAL_DIM))
            vp_rows.append(vf.reshape(bp, n_p, N_HEADS, VAL_DIM))

            q, kb, vb, kf, vf = _qkv(xs, pend_s, grp_s, i, norm_mix3, wqkv_bf[j], cos_s, sin_s, tm_s)
            r = n_s * N_HEADS
            o3 = _attn_sample(q.reshape(bs, r, VAL_DIM), kb.reshape(bs, r, VAL_DIM), vb.reshape(bs, r, VAL_DIM),
                              cache_k, cache_v, page_table, lam, sub_row, j, out_scale)
            xs = _attn_out(xs, pend_s, o3.reshape(bs * n_s, d), grp_s, i, wo_bf[j], tm_s)
            ks_rows.append(kf.reshape(bs, n_s, N_HEADS, VAL_DIM))
            vs_rows.append(vf.reshape(bs, n_s, N_HEADS, VAL_DIM))

        yp = _moe(xp, None, grp_p, i, norm_ffn3, router_wt, router_b, wg_bf, wu_bf, wd_bf, tm_p, tme_p)
        ys = _moe(xs, None, grp_s, i, norm_ffn3, router_wt, router_b, wg_bf, wu_bf, wd_bf, tm_s, tme_s)
        pend_p, pend_s = (yp, i), (ys, i)

    g_row = norm_final.reshape(1, d)
    y_prompt = _final_norm(xp, pend_p, grp_p, g_row, tm_p).reshape(bp, n_p, d)
    y_sample = _final_norm(xs, pend_s, grp_s, g_row, tm_s).reshape(bs, n_s, d)
    return (y_prompt, y_sample, jnp.stack(pool_p), jnp.stack(pool_s),
            jnp.stack(kp_rows), jnp.stack(vp_rows), jnp.stack(ks_rows), jnp.stack(vs_rows))
```

```python
import functools
import math

import jax
import jax.numpy as jnp
from jax import lax
from jax.experimental import pallas as pl
from jax.experimental.pallas import tpu as pltpu

F32 = jnp.float32
BF16 = jnp.bfloat16

D_MODEL = 1024
N_HEADS = 8
HEAD_DIM = 64
VAL_DIM = 128
ROPE_THETA = 10000.0
POOL_WINDOWS = (2, 4, 8, 16)
GROUP_CH = 256
MAX_WIN = 16
N_EXPERTS = 16
EXPERTS_PER_GROUP = 4
N_EXPERT_GROUPS = 4
D_EXPERT = 256
PAGE_SIZE = 128
EPS = 1e-6
N_MODS = 6

VMEM_LIMIT_BYTES = 52 * 1024 * 1024


def _cparams(*sem):
    return pltpu.CompilerParams(dimension_semantics=sem, vmem_limit_bytes=VMEM_LIMIT_BYTES)


def _normmod(x, g, sc, sh):
    r = lax.rsqrt(jnp.mean(x * x, axis=-1, keepdims=True) + EPS)
    return (x * r) * g * (1.0 + sc) + sh


def _dot(a, b):
    return jnp.dot(a, b, preferred_element_type=F32)


def _dot_nt(a, b):
    return lax.dot_general(a, b, (((1,), (1,)), ((), ())), preferred_element_type=F32)


def _ada_kernel(c_ref, w_ref, b_ref, os_ref, op_ref, *, n_s):
    c = c_ref[...]
    a = (c * jax.nn.sigmoid(c)).astype(BF16)
    r = _dot(a, w_ref[...].astype(BF16)) + b_ref[...]
    os_ref[...] = r[:n_s]
    op_ref[...] = r[n_s:]


def _ada_mods(c_sample, c_prompt, ada_w, ada_b, tn=768):
    depth, d, n = ada_w.shape
    n_s, n_p = c_sample.shape[0], c_prompt.shape[0]
    c_all = jnp.concatenate([c_sample, c_prompt], axis=0)
    return pl.pallas_call(
        functools.partial(_ada_kernel, n_s=n_s),
        grid=(depth, n // tn),
        in_specs=[
            pl.BlockSpec((n_s + n_p, d), lambda i, j: (0, 0)),
            pl.BlockSpec((None, d, tn), lambda i, j: (i, 0, j)),
            pl.BlockSpec((None, 1, tn), lambda i, j: (i, 0, j)),
        ],
        out_specs=[
            pl.BlockSpec((None, n_s, tn), lambda i, j: (i, 0, j)),
            pl.BlockSpec((None, n_p, tn), lambda i, j: (i, 0, j)),
        ],
        out_shape=[jax.ShapeDtypeStruct((depth, n_s, n), F32),
                   jax.ShapeDtypeStruct((depth, n_p, n), F32)],
        compiler_params=_cparams("arbitrary", "arbitrary"),
        name="ada_mods",
    )(c_all, ada_w, ada_b.reshape(depth, 1, n))


class _Group:
    def __init__(self, mods, per_token, seq):
        self.mods = mods
        self.per_token = per_token
        self.seq = seq

    def mod_spec(self, layer, chunk, tm):
        if self.per_token:
            return pl.BlockSpec((None, tm, D_MODEL), lambda t, *_: (layer, t, chunk))
        tpb = self.seq // tm
        return pl.BlockSpec((None, None, 1, D_MODEL), lambda t, *_: (layer, t // tpb, 0, chunk))


def _row_spec(layer, width=D_MODEL):
    return pl.BlockSpec((None, 1, width), lambda t, *_: (layer, 0, 0))


def _pool_p_kernel(x_ref, halo_ref, sh_ref, sc_ref, gt_ref, nrm_ref, w_ref, ps_ref,
                   xo_ref, st_ref, ext_ref, *, ts, tpb, rc):
    s = pl.program_id(0) % tpb
    g = nrm_ref[...]
    sc = sc_ref[...]
    sh = sh_ref[...]
    hh = _normmod(halo_ref[...], g, sc, sh)
    ext_ref[0:MAX_WIN, :] = jnp.where(s == 0, 0.0, hh)
    ext_ref[MAX_WIN:, :] = _normmod(x_ref[...], g, sc, sh)
    for r0 in range(0, ts, rc):
        pos = s * ts + r0 + lax.broadcasted_iota(jnp.int32, (rc, 1), 0)
        for gi, w in enumerate(POOL_WINDOWS):
            cs = slice(gi * GROUP_CH, (gi + 1) * GROUP_CH)
            base = MAX_WIN + r0
            h = ext_ref[base:base + rc, cs]
            acc = h
            for k in range(1, w):
                acc = acc + ext_ref[base - k:base - k + rc, cs]
            cnt = jnp.minimum(w, pos + 1).astype(F32)
            pooled = acc / cnt - h
            o = _dot(pooled.astype(BF16), w_ref[gi]) * ps_ref[:, cs]
            xo_ref[r0:r0 + rc, cs] = x_ref[r0:r0 + rc, cs] + gt_ref[:, cs] * o

    @pl.when(s == tpb - 1)
    def _():
        st_ref[...] = ext_ref[ts:ts + MAX_WIN, :]


def _pool_prompt(x, grp, layer, norm_mix, pool_w_bf, pool_scale, j, n_batch, ts=512, rc=128):
    t_tot, d = x.shape
    tpb = grp.seq // ts
    hb = ts // MAX_WIN
    return pl.pallas_call(
        functools.partial(_pool_p_kernel, ts=ts, tpb=tpb, rc=rc),
        grid=(t_tot // ts,),
        in_specs=[
            pl.BlockSpec((ts, d), lambda t: (t, 0)),
            pl.BlockSpec((MAX_WIN, d), lambda t: (jnp.maximum(t * hb - 1, 0), 0)),
            grp.mod_spec(layer, 0, ts), grp.mod_spec(layer, 1, ts), grp.mod_spec(layer, 2, ts),
            _row_spec(layer),
            pl.BlockSpec((None, 4, GROUP_CH, GROUP_CH), lambda t: (j, 0, 0, 0)),
            _row_spec(j),
        ],
        out_specs=[
            pl.BlockSpec((ts, d), lambda t: (t, 0)),
            pl.BlockSpec((None, MAX_WIN, d), lambda t: (t // tpb, 0, 0)),
        ],
        out_shape=[jax.ShapeDtypeStruct((t_tot, d), F32),
                   jax.ShapeDtypeStruct((n_batch, MAX_WIN, d), F32)],
        scratch_shapes=[pltpu.VMEM((ts + MAX_WIN, d), F32)],
        compiler_params=_cparams("arbitrary"),
        name="pool_prompt",
    )(x, x, grp.mods, grp.mods, grp.mods, norm_mix, pool_w_bf, pool_scale)


def _pool_s_kernel(x_ref, st_ref, sh_ref, sc_ref, gt_ref, nrm_ref, w_ref, ps_ref,
                   xo_ref, so_ref, ext_ref, *, bb, n_new):
    x = x_ref[...]
    h2 = _normmod(x, nrm_ref[...], sc_ref[...], sh_ref[...])
    ext_ref[:, 0:MAX_WIN, :] = st_ref[...]
    ext_ref[:, MAX_WIN:, :] = h2.reshape(bb, n_new, D_MODEL)
    outs = []
    for gi, w in enumerate(POOL_WINDOWS):
        cs = slice(gi * GROUP_CH, (gi + 1) * GROUP_CH)
        h = ext_ref[:, MAX_WIN:MAX_WIN + n_new, cs]
        acc = h
        for k in range(1, w):
            acc = acc + ext_ref[:, MAX_WIN - k:MAX_WIN - k + n_new, cs]
        pooled = (acc / float(w) - h).reshape(bb * n_new, GROUP_CH)
        outs.append(_dot(pooled.astype(BF16), w_ref[gi]))
    out = jnp.concatenate(outs, axis=-1) * ps_ref[...]
    xo_ref[...] = x + gt_ref[...] * out
    so_ref[...] = ext_ref[:, n_new:n_new + MAX_WIN, :]


def _pool_sample(x, state_pad, grp, layer, norm_mix, pool_w_bf, pool_scale, j, n_new, bb=32):
    t_tot, d = x.shape
    n_seq = t_tot // n_new
    tm = bb * n_new
    return pl.pallas_call(
        functools.partial(_pool_s_kernel, bb=bb, n_new=n_new),
        grid=(n_seq // bb,),
        in_specs=[
            pl.BlockSpec((tm, d), lambda t: (t, 0)),
            pl.BlockSpec((bb, MAX_WIN, d), lambda t: (t, 0, 0)),
            grp.mod_spec(layer, 0, tm), grp.mod_spec(layer, 1, tm), grp.mod_spec(layer, 2, tm),
            _row_spec(layer),
            pl.BlockSpec((None, 4, GROUP_CH, GROUP_CH), lambda t: (j, 0, 0, 0)),
            _row_spec(j),
        ],
        out_specs=[
            pl.BlockSpec((tm, d), lambda t: (t, 0)),
            pl.BlockSpec((bb, MAX_WIN, d), lambda t: (t, 0, 0)),
        ],
        out_shape=[jax.ShapeDtypeStruct((t_tot, d), F32),
                   jax.ShapeDtypeStruct((n_seq, MAX_WIN, d), F32)],
        scratch_shapes=[pltpu.VMEM((bb, MAX_WIN + n_new, d), F32)],
        compiler_params=_cparams("arbitrary"),
        name="pool_sample",
    )(x, state_pad, grp.mods, grp.mods, grp.mods, norm_mix, pool_w_bf, pool_scale)


def _qkv_kernel(x_ref, sh_ref, sc_ref, nrm_ref, w_ref, cos_ref, sin_ref,
                q_ref, kb_ref, vb_ref, kf_ref, vf_ref):
    h = _normmod(x_ref[...], nrm_ref[...], sc_ref[...], sh_ref[...]).astype(BF16)
    cos = cos_ref[...]
    sin = sin_ref[...]
    lane = lax.broadcasted_iota(jnp.int32, (1, VAL_DIM), 1)
    first_half = (lane % HEAD_DIM) < (HEAD_DIM // 2)

    def rope(xh):
        rot = jnp.where(first_half, pltpu.roll(xh, VAL_DIM - HEAD_DIM // 2, 1),
                        pltpu.roll(xh, HEAD_DIM // 2, 1))
        return xh * cos + rot * sin

    pw = 2 * VAL_DIM
    for c0 in range(0, D_MODEL, pw):
        q2 = _dot(h, w_ref[:, c0:c0 + pw])
        k2 = _dot(h, w_ref[:, D_MODEL + c0:D_MODEL + c0 + pw])
        v2 = _dot(h, w_ref[:, 2 * D_MODEL + c0:2 * D_MODEL + c0 + pw])
        for u in range(2):
            cs = slice(c0 + u * VAL_DIM, c0 + (u + 1) * VAL_DIM)
            us = slice(u * VAL_DIM, (u + 1) * VAL_DIM)
            q_ref[:, cs] = (rope(q2[:, us]) * (HEAD_DIM ** -0.5)).astype(BF16)
            kr = rope(k2[:, us])
            kf_ref[:, cs] = kr
            kb_ref[:, cs] = kr.astype(BF16)
        vf_ref[:, c0:c0 + pw] = v2
        vb_ref[:, c0:c0 + pw] = v2.astype(BF16)


def _qkv(x, grp, layer, norm_mix, wqkv_bf, cos_tab, sin_tab, tm):
    t_tot, d = x.shape
    n_pos_tiles = cos_tab.shape[0] // tm
    tok = pl.BlockSpec((tm, d), lambda t: (t, 0))
    tab = pl.BlockSpec((tm, VAL_DIM), lambda t: (t % n_pos_tiles, 0))
    return pl.pallas_call(
        _qkv_kernel,
        grid=(t_tot // tm,),
        in_specs=[tok, grp.mod_spec(layer, 0, tm), grp.mod_spec(layer, 1, tm), _row_spec(layer),
                  pl.BlockSpec((d, 3 * d), lambda t: (0, 0)), tab, tab],
        out_specs=[tok] * 5,
        out_shape=[jax.ShapeDtypeStruct((t_tot, d), BF16)] * 3 + [jax.ShapeDtypeStruct((t_tot, d), F32)] * 2,
        compiler_params=_cparams("arbitrary"),
        name="qkv_rope",
    )(x, grp.mods, grp.mods, norm_mix, wqkv_bf, cos_tab, sin_tab)


def _rope_tables(pos):
    inv = 1.0 / (ROPE_THETA ** (jnp.arange(0, HEAD_DIM, 2, dtype=F32) / HEAD_DIM))
    ang = pos.astype(F32)[:, None] * inv[None, :]
    cos, sin = jnp.cos(ang), jnp.sin(ang)
    cos_t = jnp.concatenate([cos, cos, cos, cos], axis=-1)
    sin_t = jnp.concatenate([-sin, sin, -sin, sin], axis=-1)
    return cos_t, sin_t


def _split_components(q):
    lane = lax.broadcasted_iota(jnp.int32, (1, VAL_DIM), 1)
    zero = jnp.zeros_like(q)
    return jnp.concatenate([jnp.where(lane < HEAD_DIM, q, zero), jnp.where(lane >= HEAD_DIM, q, zero)], axis=0)


def _with_ones(v):
    return jnp.concatenate([v, jnp.ones_like(v)], axis=1)


def _online_update(s, m_old, acc_old, v1):
    chunks = [s[:, u:u + VAL_DIM] for u in range(0, s.shape[1], VAL_DIM)]
    cm = chunks[0]
    for c in chunks[1:]:
        cm = jnp.maximum(cm, c)
    m_new = jnp.maximum(m_old, jnp.max(cm, axis=-1, keepdims=True))
    alpha = jnp.exp(m_old - m_new)
    p = jnp.concatenate([jnp.exp(c - m_new) for c in chunks], axis=1).astype(BF16)
    acc_new = jnp.concatenate([alpha, alpha], axis=1) * acc_old + _dot(p, v1)
    return m_new, acc_new


def _diff_finish(acc, lam, subln, out_scale, r):
    o = acc[:, :VAL_DIM] / acc[:, VAL_DIM:]
    od = o[:r] - lam * o[r:]
    y = od * lax.rsqrt(jnp.mean(od * od, axis=-1, keepdims=True) + EPS)
    return y * subln * out_scale


KEY_BLOCKS_PER_TRIP = 4

def _attn_p_kernel(lam_ref, q_ref, k_ref, v_ref, sub_ref, o_ref, qs_ref, m_ref, acc_ref, *, tq, rc, out_scale):
    qi = pl.program_id(2)
    r = 2 * tq
    qs_ref[...] = _split_components(q_ref[...])
    m_ref[...] = jnp.full(m_ref.shape, -jnp.inf, F32)
    acc_ref[...] = jnp.zeros(acc_ref.shape, F32)

    def step(kj, masked):
        off = pl.multiple_of(kj * tq, tq)
        k = k_ref[pl.ds(off, tq), :]
        v1 = _with_ones(v_ref[pl.ds(off, tq), :])
        for r0 in range(0, r, rc):
            rows = slice(r0, r0 + rc)
            nk = (r0 % tq) + rc if masked else tq
            s = _dot_nt(qs_ref[rows, :], k[:nk])
            if masked:
                row = (r0 + lax.broadcasted_iota(jnp.int32, (rc, 1), 0)) % tq
                col = lax.broadcasted_iota(jnp.int32, (1, nk), 1)
                s = jnp.where(col <= row, s, -jnp.inf)
            m_new, acc_new = _online_update(s, m_ref[rows, :], acc_ref[rows, :], v1[:nk])
            m_ref[rows, :] = m_new
            acc_ref[rows, :] = acc_new

    def body(t, carry):
        for u in range(KEY_BLOCKS_PER_TRIP):
            step(KEY_BLOCKS_PER_TRIP * t + u, False)
        return carry

    n_trips = qi // KEY_BLOCKS_PER_TRIP
    lax.fori_loop(0, n_trips, body, 0)
    left = qi - n_trips * KEY_BLOCKS_PER_TRIP

    @pl.when(left >= 2)
    def _():
        step(n_trips * KEY_BLOCKS_PER_TRIP, False)
        step(n_trips * KEY_BLOCKS_PER_TRIP + 1, False)

    @pl.when(left % 2 == 1)
    def _():
        step(qi - 1, False)

    step(qi, True)
    o_ref[...] = _diff_finish(acc_ref[...], lam_ref[0], sub_ref[...], out_scale, tq).astype(BF16)


def _attn_prompt(q, k, v, lam, subln_row, n_batch, seq, out_scale, tq=512, rc=256):
    t_tot, d = q.shape
    nq = seq // tq
    return pl.pallas_call(
        functools.partial(_attn_p_kernel, tq=tq, rc=rc, out_scale=out_scale),
        grid=(n_batch, N_HEADS, nq),
        in_specs=[
            pl.BlockSpec(memory_space=pltpu.SMEM),
            pl.BlockSpec((tq, VAL_DIM), lambda b, h, i: (b * nq + i, h)),
            pl.BlockSpec((seq, VAL_DIM), lambda b, h, i: (b, h)),
            pl.BlockSpec((seq, VAL_DIM), lambda b, h, i: (b, h)),
            pl.BlockSpec((1, VAL_DIM), lambda b, h, i: (0, 0)),
        ],
        out_specs=pl.BlockSpec((tq, VAL_DIM), lambda b, h, i: (b * nq + i, h)),
        out_shape=jax.ShapeDtypeStruct((t_tot, d), BF16),
        scratch_shapes=[pltpu.VMEM((2 * tq, VAL_DIM), BF16), pltpu.VMEM((2 * tq, VAL_DIM), F32),
                        pltpu.VMEM((2 * tq, 2 * VAL_DIM), F32)],
        compiler_params=_cparams("arbitrary", "arbitrary", "arbitrary"),
        name="attn_prompt",
    )(lam, q, k, v, subln_row)


def _attn_s_kernel(pt_ref, lam_ref, q_ref, kn_ref, vn_ref, sub_ref, *rest, n_pages, n_new, out_scale):
    del pt_ref
    k_refs = rest[:n_pages]
    v_refs = rest[n_pages:2 * n_pages]
    o_ref = rest[2 * n_pages]
    r = n_new * N_HEADS
    qs = _split_components(q_ref[...])
    row = lax.broadcasted_iota(jnp.int32, (2 * r, 1), 0)
    row_h = row % N_HEADS
    row_t = (row // N_HEADS) % n_new
    kcols = PAGE_SIZE * N_HEADS
    col_h = lax.broadcasted_iota(jnp.int32, (1, kcols), 1) % N_HEADS
    same_head = row_h == col_h

    m = jnp.full((2 * r, VAL_DIM), -jnp.inf, F32)
    acc = jnp.zeros((2 * r, 2 * VAL_DIM), F32)
    for p_i in range(n_pages):
        k = k_refs[p_i][...].reshape(kcols, VAL_DIM).astype(BF16)
        v = v_refs[p_i][...].reshape(kcols, VAL_DIM).astype(BF16)
        s = jnp.where(same_head, _dot_nt(qs, k), -jnp.inf)
        m, acc = _online_update(s, m, acc, _with_ones(v))

    ncol = lax.broadcasted_iota(jnp.int32, (1, VAL_DIM), 1)
    new_ok = (ncol < r) & (row_h == ncol % N_HEADS) & (ncol // N_HEADS <= row_t)
    pad = jnp.zeros((VAL_DIM - r, VAL_DIM), BF16)
    kn = jnp.concatenate([kn_ref[...], pad], axis=0)
    vn = jnp.concatenate([vn_ref[...], pad], axis=0)
    s = jnp.where(new_ok, _dot_nt(qs, kn), -jnp.inf)
    m, acc = _online_update(s, m, acc, _with_ones(vn))
    o_ref[...] = _diff_finish(acc, lam_ref[0], sub_ref[...], out_scale, r).astype(BF16)


def _attn_sample(q3, kn3, vn3, cache_k, cache_v, page_table, lam, subln_row, j, out_scale):
    n_seq, r, _ = q3.shape
    n_pages = page_table.shape[1]
    n_new = r // N_HEADS
    row3 = pl.BlockSpec((None, r, VAL_DIM), lambda b, pt: (b, 0, 0))

    def page_spec(p):
        return pl.BlockSpec((None, None, PAGE_SIZE, N_HEADS, VAL_DIM), lambda b, pt: (j, pt[b, p], 0, 0, 0))

    grid_spec = pltpu.PrefetchScalarGridSpec(
        num_scalar_prefetch=1,
        grid=(n_seq,),
        in_specs=[pl.BlockSpec(memory_space=pltpu.SMEM), row3, row3, row3,
                  pl.BlockSpec((1, VAL_DIM), lambda b, pt: (0, 0))]
        + [page_spec(p) for p in range(n_pages)] * 2,
        out_specs=row3,
    )
    return pl.pallas_call(
        functools.partial(_attn_s_kernel, n_pages=n_pages, n_new=n_new, out_scale=out_scale),
        grid_spec=grid_spec,
        out_shape=jax.ShapeDtypeStruct((n_seq, r, VAL_DIM), BF16),
        compiler_params=_cparams("arbitrary"),
        name="attn_sample",
    )(page_table, lam, q3, kn3, vn3, subln_row, *([cache_k] * n_pages), *([cache_v] * n_pages))


def _wo_kernel(x_ref, o_ref, gt_ref, w_ref, xo_ref):
    xo_ref[...] = x_ref[...] + gt_ref[...] * _dot(o_ref[...], w_ref[...])


def _attn_out(x, o, grp, layer, wo_bf, tm):
    t_tot, d = x.shape
    tok = pl.BlockSpec((tm, d), lambda t: (t, 0))
    return pl.pallas_call(
        _wo_kernel,
        grid=(t_tot // tm,),
        in_specs=[tok, tok, grp.mod_spec(layer, 2, tm), pl.BlockSpec((d, d), lambda t: (0, 0))],
        out_specs=tok,
        out_shape=jax.ShapeDtypeStruct((t_tot, d), F32),
        compiler_params=_cparams("arbitrary"),
        name="attn_out",
    )(x, o, grp.mods, wo_bf)


def _route(logits_t, bias):
    s_all = jax.nn.sigmoid(logits_t)
    sb_all = s_all + bias
    s = [s_all[e:e + 1, :] for e in range(N_EXPERTS)]
    sb = [sb_all[e:e + 1, :] for e in range(N_EXPERTS)]
    n = EXPERTS_PER_GROUP
    gscore = []
    for g in range(N_EXPERT_GROUPS):
        v = sb[g * n:(g + 1) * n]
        best = v[0] + v[1]
        for a in range(n):
            for b in range(a + 1, n):
                if (a, b) != (0, 1):
                    best = jnp.maximum(best, v[a] + v[b])
        gscore.append(best)
    gsel = jnp.zeros_like(gscore[0], dtype=jnp.int32)
    gbest = gscore[0]
    for g in range(1, N_EXPERT_GROUPS):
        better = gscore[g] > gbest
        gsel = jnp.where(better, g, gsel)
        gbest = jnp.where(better, gscore[g], gbest)

    def pick(rows, r):
        out = rows[r]
        for g in range(1, N_EXPERT_GROUPS):
            out = jnp.where(gsel == g, rows[g * n + r], out)
        return out

    vb = [pick(sb, r) for r in range(n)]
    vu = [pick(s, r) for r in range(n)]

    def argmax_first(vals):
        idx = jnp.zeros_like(gsel)
        best = vals[0]
        for r in range(1, n):
            better = vals[r] > best
            idx = jnp.where(better, r, idx)
            best = jnp.where(better, vals[r], best)
        return idx

    i1 = argmax_first(vb)
    i2 = argmax_first([jnp.where(i1 == r, -jnp.inf, vb[r]) for r in range(n)])

    def take(vals, idx):
        out = vals[0]
        for r in range(1, n):
            out = jnp.where(idx == r, vals[r], out)
        return out

    w1 = take(vu, i1)
    w2 = take(vu, i2)
    tot = w1 + w2
    e1 = gsel * n + i1
    e2 = gsel * n + i2
    erow = lax.broadcasted_iota(jnp.int32, logits_t.shape, 0)
    return jnp.where(erow == e1, w1 / tot, 0.0) + jnp.where(erow == e2, w2 / tot, 0.0)


def _router_logits(h2, h2_bf, rw):
    rw_hi = rw.astype(BF16)
    rw_lo = (rw - rw_hi.astype(F32)).astype(BF16)
    h_lo = (h2 - h2_bf.astype(F32)).astype(BF16)
    return _dot_nt(rw_hi, h2_bf) + (_dot_nt(rw_hi, h_lo) + _dot_nt(rw_lo, h2_bf))


def _moe_dense_kernel(x_ref, sh_ref, sc_ref, gt_ref, nrm_ref, rw_ref, rb_ref, wg_ref, wu_ref, wd_ref,
                      xo_ref, h_ref, gates_ref, acc_ref, *, tm, n_e):
    e = pl.program_id(1)

    @pl.when(e == 0)
    def _():
        h2 = _normmod(x_ref[...], nrm_ref[...], sc_ref[...], sh_ref[...])
        h2_bf = h2.astype(BF16)
        h_ref[...] = h2_bf
        gates_t = _route(_router_logits(h2, h2_bf, rw_ref[...]), rb_ref[...])
        pad = jnp.zeros((128 - N_EXPERTS, tm), F32)
        gates_ref[...] = jnp.concatenate([gates_t, pad], axis=0).T
        acc_ref[...] = jnp.zeros(acc_ref.shape, F32)

    h = h_ref[...]
    gates = gates_ref[...]
    lane = lax.broadcasted_iota(jnp.int32, (1, 128), 1)
    acts = []
    for u in range(n_e):
        hg = _dot(h, wg_ref[u])
        hu = _dot(h, wu_ref[u])
        gate = jnp.sum(jnp.where(lane == e * n_e + u, gates, 0.0), axis=-1, keepdims=True)
        acts.append(((hg * jax.nn.sigmoid(hg)) * hu * gate).astype(BF16))
    act = jnp.concatenate(acts, axis=1)
    acc_ref[...] += _dot(act, wd_ref[...].reshape(n_e * D_EXPERT, D_MODEL))

    @pl.when(e == N_EXPERTS // n_e - 1)
    def _():
        xo_ref[...] = x_ref[...] + gt_ref[...] * acc_ref[...]


def _moe_dense(x, grp, layer, norm_ffn, router_wt, router_b, wg_bf, wu_bf, wd_bf, tm, n_e=2):
    t_tot, d = x.shape
    tok = pl.BlockSpec((tm, d), lambda t, e: (t, 0))
    return pl.pallas_call(
        functools.partial(_moe_dense_kernel, tm=tm, n_e=n_e),
        grid=(t_tot // tm, N_EXPERTS // n_e),
        in_specs=[tok, grp.mod_spec(layer, 3, tm), grp.mod_spec(layer, 4, tm), grp.mod_spec(layer, 5, tm),
                  _row_spec(layer),
                  pl.BlockSpec((N_EXPERTS, d), lambda t, e: (0, 0)),
                  pl.BlockSpec((N_EXPERTS, 1), lambda t, e: (0, 0)),
                  pl.BlockSpec((None, n_e, d, D_EXPERT), lambda t, e: (layer, e, 0, 0)),
                  pl.BlockSpec((None, n_e, d, D_EXPERT), lambda t, e: (layer, e, 0, 0)),
                  pl.BlockSpec((None, n_e, D_EXPERT, d), lambda t, e: (layer, e, 0, 0))],
        out_specs=tok,
        out_shape=jax.ShapeDtypeStruct((t_tot, d), F32),
        scratch_shapes=[pltpu.VMEM((tm, d), BF16), pltpu.VMEM((tm, 128), F32),
                        pltpu.VMEM((tm, d), F32)],
        compiler_params=_cparams("arbitrary", "arbitrary"),
        name="moe_dense",
    )(x, grp.mods, grp.mods, grp.mods, norm_ffn, router_wt, router_b, wg_bf, wu_bf, wd_bf)


def _final_kernel(x_ref, g_ref, o_ref):
    x = x_ref[...]
    o_ref[...] = (x * lax.rsqrt(jnp.mean(x * x, axis=-1, keepdims=True) + EPS)) * g_ref[...]


def _final_norm(x, g_row, tm):
    t_tot, d = x.shape
    tok = pl.BlockSpec((tm, d), lambda t: (t, 0))
    return pl.pallas_call(
        _final_kernel,
        grid=(t_tot // tm,),
        in_specs=[tok, pl.BlockSpec((1, d), lambda t: (0, 0))],
        out_specs=tok,
        out_shape=jax.ShapeDtypeStruct((t_tot, d), F32),
        compiler_params=_cparams("arbitrary"),
        name="final_norm",
    )(x, g_row)


def kernel(x_prompt, x_sample, c_prompt, c_sample, state_pool, cache_k, cache_v, page_table, ada_w, ada_b, norm_mix, norm_ffn, norm_final, pool_w, pool_scale, attn_w_qkv, attn_w_o, lambda_q1, lambda_k1, lambda_q2, lambda_k2, subln, router_w, router_bias, moe_w_gate, moe_w_up, moe_w_down):
    bp, n_p, d = x_prompt.shape
    bs, n_s, _ = x_sample.shape
    depth = ada_w.shape[0]
    past_len = page_table.shape[1] * PAGE_SIZE
    tm_p, tm_s = 512, 256

    mods_s, mods_p = _ada_mods(c_sample, c_prompt, ada_w, ada_b)
    grp_p = _Group(mods_p.reshape(depth, bp, 1, N_MODS * d), per_token=False, seq=n_p)
    grp_s = _Group(jnp.repeat(mods_s, n_s, axis=1), per_token=True, seq=n_s)

    xp = x_prompt.reshape(bp * n_p, d)
    xs = x_sample.reshape(bs * n_s, d)

    norm_mix3 = norm_mix.reshape(depth, 1, d)
    norm_ffn3 = norm_ffn.reshape(depth, 1, d)
    pool_w_bf = pool_w.astype(BF16)
    pool_scale3 = pool_scale.reshape(-1, 1, d)
    wqkv_bf = attn_w_qkv.astype(BF16)
    wo_bf = attn_w_o.astype(BF16)
    router_wt = router_w.T
    router_b = router_bias.reshape(N_EXPERTS, 1)
    wg_bf = moe_w_gate.astype(BF16)
    wu_bf = moe_w_up.astype(BF16)
    wd_bf = moe_w_down.astype(BF16)

    cos_p, sin_p = _rope_tables(jnp.arange(n_p))
    cos_s, sin_s = _rope_tables(past_len + jnp.arange(n_s))
    cos_s = jnp.tile(cos_s, (tm_s // n_s, 1))
    sin_s = jnp.tile(sin_s, (tm_s // n_s, 1))

    pool_p, pool_s, kp_rows, vp_rows, ks_rows, vs_rows = [], [], [], [], [], []
    for i in range(depth):
        j = i // 2
        if i % 2 == 0:
            xp, st_p = _pool_prompt(xp, grp_p, i, norm_mix3, pool_w_bf, pool_scale3, j, bp, ts=tm_p)
            state_pad = jnp.pad(state_pool[j], ((0, 0), (1, 0), (0, 0)))
            xs, st_s = _pool_sample(xs, state_pad, grp_s, i, norm_mix3, pool_w_bf, pool_scale3, j, n_s)
            pool_p.append(st_p[:, 1:])
            pool_s.append(st_s[:, 1:])
        else:
            lam_init = 0.8 - 0.6 * math.exp(-0.3 * i)
            lam = (jnp.exp(jnp.sum(lambda_q1[j] * lambda_k1[j])) - jnp.exp(jnp.sum(lambda_q2[j] * lambda_k2[j]))
                   + lam_init).reshape(1).astype(F32)
            sub_row = subln[j].reshape(1, VAL_DIM)
            out_scale = 1.0 - lam_init

            q, kb, vb, kf, vf = _qkv(xp, grp_p, i, norm_mix3, wqkv_bf[j], cos_p, sin_p, tm_p)
            o = _attn_prompt(q, kb, vb, lam, sub_row, bp, n_p, out_scale)
            xp = _attn_out(xp, o, grp_p, i, wo_bf[j], tm_p)
            kp_rows.append(kf.reshape(bp, n_p, N_HEADS, VAL_DIM))
            vp_rows.append(vf.reshape(bp, n_p, N_HEADS, VAL_DIM))

            q, kb, vb, kf, vf = _qkv(xs, grp_s, i, norm_mix3, wqkv_bf[j], cos_s, sin_s, tm_s)
            r = n_s * N_HEADS
            o3 = _attn_sample(q.reshape(bs, r, VAL_DIM), kb.reshape(bs, r, VAL_DIM), vb.reshape(bs, r, VAL_DIM),
                              cache_k, cache_v, page_table, lam, sub_row, j, out_scale)
            xs = _attn_out(xs, o3.reshape(bs * n_s, d), grp_s, i, wo_bf[j], tm_s)
            ks_rows.append(kf.reshape(bs, n_s, N_HEADS, VAL_DIM))
            vs_rows.append(vf.reshape(bs, n_s, N_HEADS, VAL_DIM))

        xp = _moe_dense(xp, grp_p, i, norm_ffn3, router_wt, router_b, wg_bf, wu_bf, wd_bf, 2 * tm_p)
        xs = _moe_dense(xs, grp_s, i, norm_ffn3, router_wt, router_b, wg_bf, wu_bf, wd_bf, 2 * tm_s)

    g_row = norm_final.reshape(1, d)
    y_prompt = _final_norm(xp, g_row, tm_p).reshape(bp, n_p, d)
    y_sample = _final_norm(xs, g_row, tm_s).reshape(bs, n_s, d)
    return (y_prompt, y_sample, jnp.stack(pool_p), jnp.stack(pool_s),
            jnp.stack(kp_rows), jnp.stack(vp_rows), jnp.stack(ks_rows), jnp.stack(vs_rows))
```

```python
import functools
import math

import jax
import jax.numpy as jnp
from jax import lax
from jax.experimental import pallas as pl
from jax.experimental.pallas import tpu as pltpu

F32 = jnp.float32
BF16 = jnp.bfloat16

D_MODEL = 1024
N_HEADS = 8
HEAD_DIM = 64
VAL_DIM = 128
ROPE_THETA = 10000.0
POOL_WINDOWS = (2, 4, 8, 16)
GROUP_CH = 256
MAX_WIN = 16
N_EXPERTS = 16
EXPERTS_PER_GROUP = 4
N_EXPERT_GROUPS = 4
D_EXPERT = 256
PAGE_SIZE = 128
EPS = 1e-6
N_MODS = 6

VMEM_LIMIT_BYTES = 52 * 1024 * 1024


def _cparams(*sem):
    return pltpu.CompilerParams(dimension_semantics=sem, vmem_limit_bytes=VMEM_LIMIT_BYTES)


def _normmod(x, g, sc, sh):
    r = lax.rsqrt(jnp.mean(x * x, axis=-1, keepdims=True) + EPS)
    return (x * r) * g * (1.0 + sc) + sh


def _dot(a, b):
    return jnp.dot(a, b, preferred_element_type=F32)


def _dot_nt(a, b):
    return lax.dot_general(a, b, (((1,), (1,)), ((), ())), preferred_element_type=F32)


def _ada_kernel(c_ref, w_ref, b_ref, os_ref, op_ref, *, n_s):
    c = c_ref[...]
    a = (c * jax.nn.sigmoid(c)).astype(BF16)
    r = _dot(a, w_ref[...].astype(BF16)) + b_ref[...]
    os_ref[...] = r[:n_s]
    op_ref[...] = r[n_s:]


def _ada_mods(c_sample, c_prompt, ada_w, ada_b, tn=768):
    depth, d, n = ada_w.shape
    n_s, n_p = c_sample.shape[0], c_prompt.shape[0]
    c_all = jnp.concatenate([c_sample, c_prompt], axis=0)
    return pl.pallas_call(
        functools.partial(_ada_kernel, n_s=n_s),
        grid=(depth, n // tn),
        in_specs=[
            pl.BlockSpec((n_s + n_p, d), lambda i, j: (0, 0)),
            pl.BlockSpec((None, d, tn), lambda i, j: (i, 0, j)),
            pl.BlockSpec((None, 1, tn), lambda i, j: (i, 0, j)),
        ],
        out_specs=[
            pl.BlockSpec((None, n_s, tn), lambda i, j: (i, 0, j)),
            pl.BlockSpec((None, n_p, tn), lambda i, j: (i, 0, j)),
        ],
        out_shape=[jax.ShapeDtypeStruct((depth, n_s, n), F32),
                   jax.ShapeDtypeStruct((depth, n_p, n), F32)],
        compiler_params=_cparams("arbitrary", "arbitrary"),
        name="ada_mods",
    )(c_all, ada_w, ada_b.reshape(depth, 1, n))


class _Group:
    def __init__(self, mods, many_per_tile, seq):
        self.mods = mods
        self.many_per_tile = many_per_tile
        self.seq = seq

    def mod_spec(self, layer, chunk, tm):
        if self.many_per_tile:
            return pl.BlockSpec((None, tm // self.seq, D_MODEL), lambda t, *_: (layer, t, chunk))
        tpb = self.seq // tm
        return pl.BlockSpec((None, None, 1, D_MODEL), lambda t, *_: (layer, t // tpb, 0, chunk))


def _mod(ref, tm):
    m = ref[...]
    n, d = m.shape
    if n == 1 or n == tm:
        return m
    return jnp.broadcast_to(m[:, None, :], (n, tm // n, d)).reshape(tm, d)


def _row_spec(layer, width=D_MODEL):
    return pl.BlockSpec((None, 1, width), lambda t, *_: (layer, 0, 0))


def _pool_p_kernel(x_ref, halo_ref, sh_ref, sc_ref, gt_ref, nrm_ref, w_ref, ps_ref,
                   xo_ref, st_ref, ext_ref, *, ts, tpb, rc):
    s = pl.program_id(0) % tpb
    g = nrm_ref[...]
    sc = sc_ref[...]
    sh = sh_ref[...]
    hh = _normmod(halo_ref[...], g, sc, sh)
    ext_ref[0:MAX_WIN, :] = jnp.where(s == 0, 0.0, hh)
    ext_ref[MAX_WIN:, :] = _normmod(x_ref[...], g, sc, sh)
    for r0 in range(0, ts, rc):
        pos = s * ts + r0 + lax.broadcasted_iota(jnp.int32, (rc, 1), 0)
        for gi, w in enumerate(POOL_WINDOWS):
            cs = slice(gi * GROUP_CH, (gi + 1) * GROUP_CH)
            base = MAX_WIN + r0
            h = ext_ref[base:base + rc, cs]
            acc = h
            for k in range(1, w):
                acc = acc + ext_ref[base - k:base - k + rc, cs]
            cnt = jnp.minimum(w, pos + 1).astype(F32)
            pooled = acc / cnt - h
            o = _dot(pooled.astype(BF16), w_ref[gi]) * ps_ref[:, cs]
            xo_ref[r0:r0 + rc, cs] = x_ref[r0:r0 + rc, cs] + gt_ref[:, cs] * o

    @pl.when(s == tpb - 1)
    def _():
        st_ref[...] = ext_ref[ts:ts + MAX_WIN, :]


def _pool_prompt(x, grp, layer, norm_mix, pool_w_bf, pool_scale, j, n_batch, ts=512, rc=128):
    t_tot, d = x.shape
    tpb = grp.seq // ts
    hb = ts // MAX_WIN
    return pl.pallas_call(
        functools.partial(_pool_p_kernel, ts=ts, tpb=tpb, rc=rc),
        grid=(t_tot // ts,),
        in_specs=[
            pl.BlockSpec((ts, d), lambda t: (t, 0)),
            pl.BlockSpec((MAX_WIN, d), lambda t: (jnp.maximum(t * hb - 1, 0), 0)),
            grp.mod_spec(layer, 0, ts), grp.mod_spec(layer, 1, ts), grp.mod_spec(layer, 2, ts),
            _row_spec(layer),
            pl.BlockSpec((None, 4, GROUP_CH, GROUP_CH), lambda t: (j, 0, 0, 0)),
            _row_spec(j),
        ],
        out_specs=[
            pl.BlockSpec((ts, d), lambda t: (t, 0)),
            pl.BlockSpec((None, MAX_WIN, d), lambda t: (t // tpb, 0, 0)),
        ],
        out_shape=[jax.ShapeDtypeStruct((t_tot, d), F32),
                   jax.ShapeDtypeStruct((n_batch, MAX_WIN, d), F32)],
        scratch_shapes=[pltpu.VMEM((ts + MAX_WIN, d), F32)],
        compiler_params=_cparams("arbitrary"),
        name="pool_prompt",
    )(x, x, grp.mods, grp.mods, grp.mods, norm_mix, pool_w_bf, pool_scale)


def _pool_s_kernel(x_ref, st_ref, sh_ref, sc_ref, gt_ref, nrm_ref, w_ref, ps_ref,
                   xo_ref, so_ref, ext_ref, *, bb, n_new):
    x = x_ref[...]
    tm = bb * n_new
    h2 = _normmod(x, nrm_ref[...], _mod(sc_ref, tm), _mod(sh_ref, tm))
    ext_ref[:, 0:MAX_WIN, :] = st_ref[...]
    ext_ref[:, MAX_WIN:, :] = h2.reshape(bb, n_new, D_MODEL)
    outs = []
    for gi, w in enumerate(POOL_WINDOWS):
        cs = slice(gi * GROUP_CH, (gi + 1) * GROUP_CH)
        h = ext_ref[:, MAX_WIN:MAX_WIN + n_new, cs]
        acc = h
        for k in range(1, w):
            acc = acc + ext_ref[:, MAX_WIN - k:MAX_WIN - k + n_new, cs]
        pooled = (acc / float(w) - h).reshape(bb * n_new, GROUP_CH)
        outs.append(_dot(pooled.astype(BF16), w_ref[gi]))
    out = jnp.concatenate(outs, axis=-1) * ps_ref[...]
    xo_ref[...] = x + _mod(gt_ref, tm) * out
    so_ref[...] = ext_ref[:, n_new:n_new + MAX_WIN, :]


def _pool_sample(x, state_pad, grp, layer, norm_mix, pool_w_bf, pool_scale, j, n_new, bb=32):
    t_tot, d = x.shape
    n_seq = t_tot // n_new
    tm = bb * n_new
    return pl.pallas_call(
        functools.partial(_pool_s_kernel, bb=bb, n_new=n_new),
        grid=(n_seq // bb,),
        in_specs=[
            pl.BlockSpec((tm, d), lambda t: (t, 0)),
            pl.BlockSpec((bb, MAX_WIN, d), lambda t: (t, 0, 0)),
            grp.mod_spec(layer, 0, tm), grp.mod_spec(layer, 1, tm), grp.mod_spec(layer, 2, tm),
            _row_spec(layer),
            pl.BlockSpec((None, 4, GROUP_CH, GROUP_CH), lambda t: (j, 0, 0, 0)),
            _row_spec(j),
        ],
        out_specs=[
            pl.BlockSpec((tm, d), lambda t: (t, 0)),
            pl.BlockSpec((bb, MAX_WIN, d), lambda t: (t, 0, 0)),
        ],
        out_shape=[jax.ShapeDtypeStruct((t_tot, d), F32),
                   jax.ShapeDtypeStruct((n_seq, MAX_WIN, d), F32)],
        scratch_shapes=[pltpu.VMEM((bb, MAX_WIN + n_new, d), F32)],
        compiler_params=_cparams("arbitrary"),
        name="pool_sample",
    )(x, state_pad, grp.mods, grp.mods, grp.mods, norm_mix, pool_w_bf, pool_scale)


def _store_head_rows(ref, head, value):
    ref[pl.ds(head, value.shape[0], stride=N_HEADS), :] = value


def _qkv_kernel(x_ref, sh_ref, sc_ref, nrm_ref, w_ref, cos_ref, sin_ref, *rest, n_prev):
    if n_prev:
        kprev_ref, vprev_ref, rest = rest[0], rest[1], rest[2:]
    q_ref, kb_ref, vb_ref, kf_ref, vf_ref = rest
    tm = x_ref.shape[0]
    if n_prev:
        kf_ref[0:n_prev] = kprev_ref[...]
        vf_ref[0:n_prev] = vprev_ref[...]
    kf_new = kf_ref.at[n_prev]
    vf_new = vf_ref.at[n_prev]
    h = _normmod(x_ref[...], nrm_ref[...], _mod(sc_ref, tm), _mod(sh_ref, tm)).astype(BF16)
    cos = cos_ref[...]
    sin = sin_ref[...]
    lane = lax.broadcasted_iota(jnp.int32, (1, VAL_DIM), 1)
    first_half = (lane % HEAD_DIM) < (HEAD_DIM // 2)

    def rope(xh):
        rot = jnp.where(first_half, pltpu.roll(xh, VAL_DIM - HEAD_DIM // 2, 1),
                        pltpu.roll(xh, HEAD_DIM // 2, 1))
        return xh * cos + rot * sin

    pw = 2 * VAL_DIM
    for c0 in range(0, D_MODEL, pw):
        q2 = _dot(h, w_ref[:, c0:c0 + pw])
        k2 = _dot(h, w_ref[:, D_MODEL + c0:D_MODEL + c0 + pw])
        v2 = _dot(h, w_ref[:, 2 * D_MODEL + c0:2 * D_MODEL + c0 + pw])
        for u in range(2):
            cs = slice(c0 + u * VAL_DIM, c0 + (u + 1) * VAL_DIM)
            us = slice(u * VAL_DIM, (u + 1) * VAL_DIM)
            q_ref[:, cs] = (rope(q2[:, us]) * (HEAD_DIM ** -0.5)).astype(BF16)
            kr = rope(k2[:, us])
            head = (c0 + u * VAL_DIM) // VAL_DIM
            _store_head_rows(kf_new, head, kr)
            _store_head_rows(vf_new, head, v2[:, us])
            kb_ref[:, cs] = kr.astype(BF16)
        vb_ref[:, c0:c0 + pw] = v2.astype(BF16)


def _qkv(x, grp, layer, norm_mix, wqkv_bf, cos_tab, sin_tab, tm, prev_rows=None):
    t_tot, d = x.shape
    n_pos_tiles = cos_tab.shape[0] // tm
    n_prev = 0 if prev_rows is None else prev_rows[0].shape[0]
    tok = pl.BlockSpec((tm, d), lambda t: (t, 0))
    tab = pl.BlockSpec((tm, VAL_DIM), lambda t: (t % n_pos_tiles, 0))

    def rows_spec(n):
        return pl.BlockSpec((n, tm * N_HEADS, VAL_DIM), lambda t: (0, t, 0))

    rows_shape = jax.ShapeDtypeStruct((n_prev + 1, t_tot * N_HEADS, VAL_DIM), F32)
    return pl.pallas_call(
        functools.partial(_qkv_kernel, n_prev=n_prev),
        grid=(t_tot // tm,),
        in_specs=[tok, grp.mod_spec(layer, 0, tm), grp.mod_spec(layer, 1, tm), _row_spec(layer),
                  pl.BlockSpec((d, 3 * d), lambda t: (0, 0), pipeline_mode=pl.Buffered(1)), tab, tab]
        + ([rows_spec(n_prev)] * 2 if n_prev else []),
        out_specs=[tok] * 3 + [rows_spec(n_prev + 1)] * 2,
        out_shape=[jax.ShapeDtypeStruct((t_tot, d), BF16)] * 3 + [rows_shape] * 2,
        compiler_params=_cparams("arbitrary"),
        name="qkv_rope",
    )(x, grp.mods, grp.mods, norm_mix, wqkv_bf, cos_tab, sin_tab, *(prev_rows or ()))


def _rope_tables(pos):
    inv = 1.0 / (ROPE_THETA ** (jnp.arange(0, HEAD_DIM, 2, dtype=F32) / HEAD_DIM))
    ang = pos.astype(F32)[:, None] * inv[None, :]
    cos, sin = jnp.cos(ang), jnp.sin(ang)
    cos_t = jnp.concatenate([cos, cos, cos, cos], axis=-1)
    sin_t = jnp.concatenate([-sin, sin, -sin, sin], axis=-1)
    return cos_t, sin_t


def _split_components(q):
    lane = lax.broadcasted_iota(jnp.int32, (1, VAL_DIM), 1)
    zero = jnp.zeros_like(q)
    return jnp.concatenate([jnp.where(lane < HEAD_DIM, q, zero), jnp.where(lane >= HEAD_DIM, q, zero)], axis=0)


def _with_ones(v):
    return jnp.concatenate([v, jnp.ones_like(v)], axis=1)


def _online_update(s, m_old, acc_old, v1):
    chunks = [s[:, u:u + VAL_DIM] for u in range(0, s.shape[1], VAL_DIM)]
    cm = chunks[0]
    for c in chunks[1:]:
        cm = jnp.maximum(cm, c)
    m_new = jnp.maximum(m_old, jnp.max(cm, axis=-1, keepdims=True))
    alpha = jnp.exp(m_old - m_new)
    p = jnp.concatenate([jnp.exp(c - m_new) for c in chunks], axis=1).astype(BF16)
    acc_new = jnp.concatenate([alpha, alpha], axis=1) * acc_old + _dot(p, v1)
    return m_new, acc_new


def _diff_finish(acc, lam, subln, out_scale, r):
    o = acc[:, :VAL_DIM] / acc[:, VAL_DIM:]
    od = o[:r] - lam * o[r:]
    y = od * lax.rsqrt(jnp.mean(od * od, axis=-1, keepdims=True) + EPS)
    return y * subln * out_scale


KEY_BLOCKS_PER_TRIP = 4

def _attn_p_kernel(lam_ref, q_ref, k_ref, v_ref, sub_ref, o_ref, qs_ref, m_ref, acc_ref, *, tq, rc, out_scale):
    qi = pl.program_id(2)
    r = 2 * tq
    qs_ref[...] = _split_components(q_ref[...])
    m_ref[...] = jnp.full(m_ref.shape, -jnp.inf, F32)
    acc_ref[...] = jnp.zeros(acc_ref.shape, F32)

    def step(kj, masked):
        off = pl.multiple_of(kj * tq, tq)
        k = k_ref[pl.ds(off, tq), :]
        v1 = _with_ones(v_ref[pl.ds(off, tq), :])
        for r0 in range(0, r, rc):
            rows = slice(r0, r0 + rc)
            nk = (r0 % tq) + rc if masked else tq
            s = _dot_nt(qs_ref[rows, :], k[:nk])
            if masked:
                row = (r0 + lax.broadcasted_iota(jnp.int32, (rc, 1), 0)) % tq
                col = lax.broadcasted_iota(jnp.int32, (1, nk), 1)
                s = jnp.where(col <= row, s, -jnp.inf)
            m_new, acc_new = _online_update(s, m_ref[rows, :], acc_ref[rows, :], v1[:nk])
            m_ref[rows, :] = m_new
            acc_ref[rows, :] = acc_new

    def body(t, carry):
        for u in range(KEY_BLOCKS_PER_TRIP):
            step(KEY_BLOCKS_PER_TRIP * t + u, False)
        return carry

    n_trips = qi // KEY_BLOCKS_PER_TRIP
    lax.fori_loop(0, n_trips, body, 0)
    left = qi - n_trips * KEY_BLOCKS_PER_TRIP

    @pl.when(left >= 2)
    def _():
        step(n_trips * KEY_BLOCKS_PER_TRIP, False)
        step(n_trips * KEY_BLOCKS_PER_TRIP + 1, False)

    @pl.when(left % 2 == 1)
    def _():
        step(qi - 1, False)
        step(qi, True)

    @pl.when(left % 2 == 0)
    def _():
        step(qi, True)
    o_ref[...] = _diff_finish(acc_ref[...], lam_ref[0], sub_ref[...], out_scale, tq).astype(BF16)


def _attn_prompt(q, k, v, lam, subln_row, n_batch, seq, out_scale, tq=512, rc=256):
    t_tot, d = q.shape
    nq = seq // tq
    return pl.pallas_call(
        functools.partial(_attn_p_kernel, tq=tq, rc=rc, out_scale=out_scale),
        grid=(n_batch, N_HEADS, nq),
        in_specs=[
            pl.BlockSpec(memory_space=pltpu.SMEM),
            pl.BlockSpec((tq, VAL_DIM), lambda b, h, i: (b * nq + i, h)),
            pl.BlockSpec((seq, VAL_DIM), lambda b, h, i: (b, h)),
            pl.BlockSpec((seq, VAL_DIM), lambda b, h, i: (b, h)),
            pl.BlockSpec((1, VAL_DIM), lambda b, h, i: (0, 0)),
        ],
        out_specs=pl.BlockSpec((tq, VAL_DIM), lambda b, h, i: (b * nq + i, h)),
        out_shape=jax.ShapeDtypeStruct((t_tot, d), BF16),
        scratch_shapes=[pltpu.VMEM((2 * tq, VAL_DIM), BF16), pltpu.VMEM((2 * tq, VAL_DIM), F32),
                        pltpu.VMEM((2 * tq, 2 * VAL_DIM), F32)],
        compiler_params=_cparams("arbitrary", "arbitrary", "arbitrary"),
        name="attn_prompt",
    )(lam, q, k, v, subln_row)


def _attn_s_kernel(pt_ref, lam_ref, q_ref, kn_ref, vn_ref, sub_ref, *rest, n_pages, n_new, out_scale):
    del pt_ref
    k_refs = rest[:n_pages]
    v_refs = rest[n_pages:2 * n_pages]
    o_ref = rest[2 * n_pages]
    r = n_new * N_HEADS
    qs = _split_components(q_ref[...])
    row = lax.broadcasted_iota(jnp.int32, (2 * r, 1), 0)
    row_h = row % N_HEADS
    row_t = (row // N_HEADS) % n_new
    kcols = PAGE_SIZE * N_HEADS
    col_h = lax.broadcasted_iota(jnp.int32, (1, kcols), 1) % N_HEADS
    same_head = row_h == col_h

    m = jnp.full((2 * r, VAL_DIM), -jnp.inf, F32)
    acc = jnp.zeros((2 * r, 2 * VAL_DIM), F32)
    for p_i in range(n_pages):
        k = k_refs[p_i][...].reshape(kcols, VAL_DIM).astype(BF16)
        v = v_refs[p_i][...].reshape(kcols, VAL_DIM).astype(BF16)
        s = jnp.where(same_head, _dot_nt(qs, k), -jnp.inf)
        m, acc = _online_update(s, m, acc, _with_ones(v))

    ncol = lax.broadcasted_iota(jnp.int32, (1, VAL_DIM), 1)
    new_ok = (ncol < r) & (row_h == ncol % N_HEADS) & (ncol // N_HEADS <= row_t)
    pad = jnp.zeros((VAL_DIM - r, VAL_DIM), BF16)
    kn = jnp.concatenate([kn_ref[...], pad], axis=0)
    vn = jnp.concatenate([vn_ref[...], pad], axis=0)
    s = jnp.where(new_ok, _dot_nt(qs, kn), -jnp.inf)
    m, acc = _online_update(s, m, acc, _with_ones(vn))
    o_ref[...] = _diff_finish(acc, lam_ref[0], sub_ref[...], out_scale, r).astype(BF16)


def _attn_sample(q3, kn3, vn3, cache_k, cache_v, page_table, lam, subln_row, j, out_scale):
    n_seq, r, _ = q3.shape
    n_pages = page_table.shape[1]
    n_new = r // N_HEADS
    row3 = pl.BlockSpec((None, r, VAL_DIM), lambda b, pt: (b, 0, 0))

    def page_spec(p):
        return pl.BlockSpec((None, None, PAGE_SIZE, N_HEADS, VAL_DIM), lambda b, pt: (j, pt[b, p], 0, 0, 0))

    grid_spec = pltpu.PrefetchScalarGridSpec(
        num_scalar_prefetch=1,
        grid=(n_seq,),
        in_specs=[pl.BlockSpec(memory_space=pltpu.SMEM), row3, row3, row3,
                  pl.BlockSpec((1, VAL_DIM), lambda b, pt: (0, 0))]
        + [page_spec(p) for p in range(n_pages)] * 2,
        out_specs=row3,
    )
    return pl.pallas_call(
        functools.partial(_attn_s_kernel, n_pages=n_pages, n_new=n_new, out_scale=out_scale),
        grid_spec=grid_spec,
        out_shape=jax.ShapeDtypeStruct((n_seq, r, VAL_DIM), BF16),
        compiler_params=_cparams("arbitrary"),
        name="attn_sample",
    )(page_table, lam, q3, kn3, vn3, subln_row, *([cache_k] * n_pages), *([cache_v] * n_pages))


def _wo_kernel(x_ref, o_ref, gt_ref, w_ref, xo_ref):
    xo_ref[...] = x_ref[...] + _mod(gt_ref, x_ref.shape[0]) * _dot(o_ref[...], w_ref[...])


def _attn_out(x, o, grp, layer, wo_bf, tm):
    t_tot, d = x.shape
    tok = pl.BlockSpec((tm, d), lambda t: (t, 0))
    return pl.pallas_call(
        _wo_kernel,
        grid=(t_tot // tm,),
        in_specs=[tok, tok, grp.mod_spec(layer, 2, tm), pl.BlockSpec((d, d), lambda t: (0, 0))],
        out_specs=tok,
        out_shape=jax.ShapeDtypeStruct((t_tot, d), F32),
        compiler_params=_cparams("arbitrary"),
        name="attn_out",
    )(x, o, grp.mods, wo_bf)


def _route(logits_t, bias):
    s_all = jax.nn.sigmoid(logits_t)
    sb_all = s_all + bias
    s = [s_all[e:e + 1, :] for e in range(N_EXPERTS)]
    sb = [sb_all[e:e + 1, :] for e in range(N_EXPERTS)]
    n = EXPERTS_PER_GROUP
    gscore = []
    for g in range(N_EXPERT_GROUPS):
        v = sb[g * n:(g + 1) * n]
        best = v[0] + v[1]
        for a in range(n):
            for b in range(a + 1, n):
                if (a, b) != (0, 1):
                    best = jnp.maximum(best, v[a] + v[b])
        gscore.append(best)
    gsel = jnp.zeros_like(gscore[0], dtype=jnp.int32)
    gbest = gscore[0]
    for g in range(1, N_EXPERT_GROUPS):
        better = gscore[g] > gbest
        gsel = jnp.where(better, g, gsel)
        gbest = jnp.where(better, gscore[g], gbest)

    def pick(rows, r):
        out = rows[r]
        for g in range(1, N_EXPERT_GROUPS):
            out = jnp.where(gsel == g, rows[g * n + r], out)
        return out

    vb = [pick(sb, r) for r in range(n)]
    vu = [pick(s, r) for r in range(n)]

    def argmax_first(vals):
        idx = jnp.zeros_like(gsel)
        best = vals[0]
        for r in range(1, n):
            better = vals[r] > best
            idx = jnp.where(better, r, idx)
            best = jnp.where(better, vals[r], best)
        return idx

    i1 = argmax_first(vb)
    i2 = argmax_first([jnp.where(i1 == r, -jnp.inf, vb[r]) for r in range(n)])

    def take(vals, idx):
        out = vals[0]
        for r in range(1, n):
            out = jnp.where(idx == r, vals[r], out)
        return out

    w1 = take(vu, i1)
    w2 = take(vu, i2)
    tot = w1 + w2
    e1 = gsel * n + i1
    e2 = gsel * n + i2
    erow = lax.broadcasted_iota(jnp.int32, logits_t.shape, 0)
    return jnp.where(erow == e1, w1 / tot, 0.0) + jnp.where(erow == e2, w2 / tot, 0.0)


def _router_logits(h2, h2_bf, rw):
    rw_hi = rw.astype(BF16)
    rw_lo = (rw - rw_hi.astype(F32)).astype(BF16)
    h_lo = (h2 - h2_bf.astype(F32)).astype(BF16)
    return _dot_nt(rw_hi, h2_bf) + (_dot_nt(rw_hi, h_lo) + _dot_nt(rw_lo, h2_bf))


def _moe_dense_kernel(x_ref, sh_ref, sc_ref, gt_ref, nrm_ref, rw_ref, rb_ref, wg_ref, wu_ref, wd_ref, *rest,
                      tm, n_e, final):
    fg_ref = rest[0] if final else None
    xo_ref, h_ref, gates_ref, acc_ref = rest[1:] if final else rest
    e = pl.program_id(1)

    @pl.when(e == 0)
    def _():
        h2 = _normmod(x_ref[...], nrm_ref[...], _mod(sc_ref, tm), _mod(sh_ref, tm))
        h2_bf = h2.astype(BF16)
        h_ref[...] = h2_bf
        gates_t = _route(_router_logits(h2, h2_bf, rw_ref[...]), rb_ref[...])
        pad = jnp.zeros((128 - N_EXPERTS, tm), F32)
        gates_ref[...] = jnp.concatenate([gates_t, pad], axis=0).T
        acc_ref[...] = jnp.zeros(acc_ref.shape, F32)

    h = h_ref[...]
    gates = gates_ref[...]
    lane = lax.broadcasted_iota(jnp.int32, (1, 128), 1)
    acts = []
    for u in range(n_e):
        hg = _dot(h, wg_ref[u])
        hu = _dot(h, wu_ref[u])
        gate = jnp.sum(jnp.where(lane == e * n_e + u, gates, 0.0), axis=-1, keepdims=True)
        acts.append(((hg * jax.nn.sigmoid(hg)) * hu * gate).astype(BF16))
    act = jnp.concatenate(acts, axis=1)
    acc_ref[...] += _dot(act, wd_ref[...].reshape(n_e * D_EXPERT, D_MODEL))

    @pl.when(e == N_EXPERTS // n_e - 1)
    def _():
        xo = x_ref[...] + _mod(gt_ref, tm) * acc_ref[...]
        if final:
            xo = (xo * lax.rsqrt(jnp.mean(xo * xo, axis=-1, keepdims=True) + EPS)) * fg_ref[...]
        xo_ref[...] = xo


def _moe_dense(x, grp, layer, norm_ffn, router_wt, router_b, wg_bf, wu_bf, wd_bf, tm, final_gain=None, n_e=2):
    t_tot, d = x.shape
    tok = pl.BlockSpec((tm, d), lambda t, e: (t, 0))
    final = final_gain is not None
    extra_specs, extra_args = ([pl.BlockSpec((1, d), lambda t, e: (0, 0))], [final_gain]) if final else ([], [])
    return pl.pallas_call(
        functools.partial(_moe_dense_kernel, tm=tm, n_e=n_e, final=final),
        grid=(t_tot // tm, N_EXPERTS // n_e),
        in_specs=[tok, grp.mod_spec(layer, 3, tm), grp.mod_spec(layer, 4, tm), grp.mod_spec(layer, 5, tm),
                  _row_spec(layer),
                  pl.BlockSpec((N_EXPERTS, d), lambda t, e: (0, 0)),
                  pl.BlockSpec((N_EXPERTS, 1), lambda t, e: (0, 0)),
                  pl.BlockSpec((None, n_e, d, D_EXPERT), lambda t, e: (layer, e, 0, 0)),
                  pl.BlockSpec((None, n_e, d, D_EXPERT), lambda t, e: (layer, e, 0, 0)),
                  pl.BlockSpec((None, n_e, D_EXPERT, d), lambda t, e: (layer, e, 0, 0))] + extra_specs,
        out_specs=tok,
        out_shape=jax.ShapeDtypeStruct((t_tot, d), F32),
        scratch_shapes=[pltpu.VMEM((tm, d), BF16), pltpu.VMEM((tm, 128), F32),
                        pltpu.VMEM((tm, d), F32)],
        compiler_params=_cparams("arbitrary", "arbitrary"),
        name="moe_dense",
    )(x, grp.mods, grp.mods, grp.mods, norm_ffn, router_wt, router_b, wg_bf, wu_bf, wd_bf, *extra_args)


def kernel(x_prompt, x_sample, c_prompt, c_sample, state_pool, cache_k, cache_v, page_table, ada_w, ada_b, norm_mix, norm_ffn, norm_final, pool_w, pool_scale, attn_w_qkv, attn_w_o, lambda_q1, lambda_k1, lambda_q2, lambda_k2, subln, router_w, router_bias, moe_w_gate, moe_w_up, moe_w_down):
    bp, n_p, d = x_prompt.shape
    bs, n_s, _ = x_sample.shape
    depth = ada_w.shape[0]
    past_len = page_table.shape[1] * PAGE_SIZE
    tm_p, tm_s = 512, 256

    mods_s, mods_p = _ada_mods(c_sample, c_prompt, ada_w, ada_b)
    grp_p = _Group(mods_p.reshape(depth, bp, 1, N_MODS * d), many_per_tile=False, seq=n_p)
    grp_s = _Group(mods_s, many_per_tile=True, seq=n_s)

    xp = x_prompt.reshape(bp * n_p, d)
    xs = x_sample.reshape(bs * n_s, d)

    norm_mix3 = norm_mix.reshape(depth, 1, d)
    norm_ffn3 = norm_ffn.reshape(depth, 1, d)
    pool_w_bf = pool_w.astype(BF16)
    pool_scale3 = pool_scale.reshape(-1, 1, d)
    wqkv_bf = attn_w_qkv.astype(BF16)
    wo_bf = attn_w_o.astype(BF16)
    router_wt = router_w.T
    router_b = router_bias.reshape(N_EXPERTS, 1)
    wg_bf = moe_w_gate.astype(BF16)
    wu_bf = moe_w_up.astype(BF16)
    wd_bf = moe_w_down.astype(BF16)

    cos_p, sin_p = _rope_tables(jnp.arange(n_p))
    cos_s, sin_s = _rope_tables(past_len + jnp.arange(n_s))
    cos_s = jnp.tile(cos_s, (tm_s // n_s, 1))
    sin_s = jnp.tile(sin_s, (tm_s // n_s, 1))

    pool_p, pool_s = [], []
    rows_p = rows_s = None
    g_row = norm_final.reshape(1, d)
    for i in range(depth):
        j = i // 2
        if i % 2 == 0:
            xp, st_p = _pool_prompt(xp, grp_p, i, norm_mix3, pool_w_bf, pool_scale3, j, bp, ts=tm_p)
            state_pad = jnp.pad(state_pool[j], ((0, 0), (1, 0), (0, 0)))
            xs, st_s = _pool_sample(xs, state_pad, grp_s, i, norm_mix3, pool_w_bf, pool_scale3, j, n_s)
            pool_p.append(st_p[:, 1:])
            pool_s.append(st_s[:, 1:])
        else:
            lam_init = 0.8 - 0.6 * math.exp(-0.3 * i)
            lam = (jnp.exp(jnp.sum(lambda_q1[j] * lambda_k1[j])) - jnp.exp(jnp.sum(lambda_q2[j] * lambda_k2[j]))
                   + lam_init).reshape(1).astype(F32)
            sub_row = subln[j].reshape(1, VAL_DIM)
            out_scale = 1.0 - lam_init

            q, kb, vb, *rows_p = _qkv(xp, grp_p, i, norm_mix3, wqkv_bf[j], cos_p, sin_p, tm_p, rows_p)
            o = _attn_prompt(q, kb, vb, lam, sub_row, bp, n_p, out_scale)
            xp = _attn_out(xp, o, grp_p, i, wo_bf[j], tm_p)

            q, kb, vb, *rows_s = _qkv(xs, grp_s, i, norm_mix3, wqkv_bf[j], cos_s, sin_s, tm_s, rows_s)
            r = n_s * N_HEADS
            o3 = _attn_sample(q.reshape(bs, r, VAL_DIM), kb.reshape(bs, r, VAL_DIM), vb.reshape(bs, r, VAL_DIM),
                              cache_k, cache_v, page_table, lam, sub_row, j, out_scale)
            xs = _attn_out(xs, o3.reshape(bs * n_s, d), grp_s, i, wo_bf[j], tm_s)

        final_gain = g_row if i == depth - 1 else None
        xp = _moe_dense(xp, grp_p, i, norm_ffn3, router_wt, router_b, wg_bf, wu_bf, wd_bf, 2 * tm_p, final_gain)
        xs = _moe_dense(xs, grp_s, i, norm_ffn3, router_wt, router_b, wg_bf, wu_bf, wd_bf, 2 * tm_s, final_gain)

    n_attn = depth // 2
    return (xp.reshape(bp, n_p, d), xs.reshape(bs, n_s, d), jnp.stack(pool_p), jnp.stack(pool_s),
            rows_p[0].reshape(n_attn, bp, n_p, N_HEADS, VAL_DIM), rows_p[1].reshape(n_attn, bp, n_p, N_HEADS, VAL_DIM),
            rows_s[0].reshape(n_attn, bs, n_s, N_HEADS, VAL_DIM), rows_s[1].reshape(n_attn, bs, n_s, N_HEADS, VAL_DIM))
```

```python
import functools
import math

import jax
import jax.numpy as jnp
from jax import lax
from jax.experimental import pallas as pl
from jax.experimental.pallas import tpu as pltpu

F32 = jnp.float32
BF16 = jnp.bfloat16

D_MODEL = 1024
N_HEADS = 8
HEAD_DIM = 64
VAL_DIM = 128
ROPE_THETA = 10000.0
POOL_WINDOWS = (2, 4, 8, 16)
GROUP_CH = 256
MAX_WIN = 16
N_EXPERTS = 16
EXPERTS_PER_GROUP = 4
N_EXPERT_GROUPS = 4
D_EXPERT = 256
PAGE_SIZE = 128
EPS = 1e-6
N_MODS = 6

VMEM_LIMIT_BYTES = 52 * 1024 * 1024


def _cparams(*sem):
    return pltpu.CompilerParams(dimension_semantics=sem, vmem_limit_bytes=VMEM_LIMIT_BYTES)


def _normmod(x, g, sc, sh):
    r = lax.rsqrt(jnp.mean(x * x, axis=-1, keepdims=True) + EPS)
    return (x * r) * g * (1.0 + sc) + sh


def _dot(a, b):
    return jnp.dot(a, b, preferred_element_type=F32)


def _dot_nt(a, b):
    return lax.dot_general(a, b, (((1,), (1,)), ((), ())), preferred_element_type=F32)


def _ada_kernel(c_ref, w_ref, b_ref, os_ref, op_ref, *, n_s):
    c = c_ref[...]
    a = (c * jax.nn.sigmoid(c)).astype(BF16)
    r = _dot(a, w_ref[...].astype(BF16)) + b_ref[...]
    os_ref[...] = r[:n_s]
    op_ref[...] = r[n_s:]


def _ada_mods(c_sample, c_prompt, ada_w, ada_b, tn=1536):
    depth, d, n = ada_w.shape
    n_s, n_p = c_sample.shape[0], c_prompt.shape[0]
    c_all = jnp.concatenate([c_sample, c_prompt], axis=0)
    return pl.pallas_call(
        functools.partial(_ada_kernel, n_s=n_s),
        grid=(depth, n // tn),
        in_specs=[
            pl.BlockSpec((n_s + n_p, d), lambda i, j: (0, 0)),
            pl.BlockSpec((None, d, tn), lambda i, j: (i, 0, j)),
            pl.BlockSpec((None, 1, tn), lambda i, j: (i, 0, j)),
        ],
        out_specs=[
            pl.BlockSpec((None, n_s, tn), lambda i, j: (i, 0, j)),
            pl.BlockSpec((None, n_p, tn), lambda i, j: (i, 0, j)),
        ],
        out_shape=[jax.ShapeDtypeStruct((depth, n_s, n), F32),
                   jax.ShapeDtypeStruct((depth, n_p, n), F32)],
        compiler_params=_cparams("arbitrary", "arbitrary"),
        name="ada_mods",
    )(c_all, ada_w, ada_b.reshape(depth, 1, n))


class _Group:
    def __init__(self, mods, many_per_tile, seq):
        self.mods = mods
        self.many_per_tile = many_per_tile
        self.seq = seq

    def mod_spec(self, layer, chunk, tm):
        if self.many_per_tile:
            return pl.BlockSpec((None, tm // self.seq, D_MODEL), lambda t, *_: (layer, t, chunk))
        tpb = self.seq // tm
        return pl.BlockSpec((None, None, 1, D_MODEL), lambda t, *_: (layer, t // tpb, 0, chunk))


def _mod(ref, tm):
    m = ref[...]
    n, d = m.shape
    if n == 1 or n == tm:
        return m
    return jnp.broadcast_to(m[:, None, :], (n, tm // n, d)).reshape(tm, d)


def _row_spec(layer, width=D_MODEL):
    return pl.BlockSpec((None, 1, width), lambda t, *_: (layer, 0, 0))


def _pool_p_kernel(x_ref, halo_ref, sh_ref, sc_ref, gt_ref, nrm_ref, w_ref, ps_ref,
                   xo_ref, st_ref, ext_ref, *, ts, tpb, rc):
    s = pl.program_id(0) % tpb
    g = nrm_ref[...]
    sc = sc_ref[...]
    sh = sh_ref[...]
    hh = _normmod(halo_ref[...], g, sc, sh)
    ext_ref[0:MAX_WIN, :] = jnp.where(s == 0, 0.0, hh)
    ext_ref[MAX_WIN:, :] = _normmod(x_ref[...], g, sc, sh)
    for r0 in range(0, ts, rc):
        pos = s * ts + r0 + lax.broadcasted_iota(jnp.int32, (rc, 1), 0)
        for gi, w in enumerate(POOL_WINDOWS):
            cs = slice(gi * GROUP_CH, (gi + 1) * GROUP_CH)
            acc = ext_ref[r0:r0 + MAX_WIN + rc, cs]
            h = acc[MAX_WIN:]
            span = 1
            while span < w:
                acc = acc + pltpu.roll(acc, span, 0)
                span *= 2
            cnt = jnp.minimum(w, pos + 1).astype(F32)
            pooled = acc[MAX_WIN:] / cnt - h
            o = _dot(pooled.astype(BF16), w_ref[gi]) * ps_ref[:, cs]
            xo_ref[r0:r0 + rc, cs] = x_ref[r0:r0 + rc, cs] + gt_ref[:, cs] * o

    @pl.when(s == tpb - 1)
    def _():
        st_ref[...] = ext_ref[ts:ts + MAX_WIN, :]


def _pool_prompt(x, grp, layer, norm_mix, pool_w_bf, pool_scale, j, n_batch, ts=512, rc=128):
    t_tot, d = x.shape
    tpb = grp.seq // ts
    hb = ts // MAX_WIN
    return pl.pallas_call(
        functools.partial(_pool_p_kernel, ts=ts, tpb=tpb, rc=rc),
        grid=(t_tot // ts,),
        in_specs=[
            pl.BlockSpec((ts, d), lambda t: (t, 0)),
            pl.BlockSpec((MAX_WIN, d), lambda t: (jnp.maximum(t * hb - 1, 0), 0)),
            grp.mod_spec(layer, 0, ts), grp.mod_spec(layer, 1, ts), grp.mod_spec(layer, 2, ts),
            _row_spec(layer),
            pl.BlockSpec((None, 4, GROUP_CH, GROUP_CH), lambda t: (j, 0, 0, 0)),
            _row_spec(j),
        ],
        out_specs=[
            pl.BlockSpec((ts, d), lambda t: (t, 0)),
            pl.BlockSpec((None, MAX_WIN, d), lambda t: (t // tpb, 0, 0)),
        ],
        out_shape=[jax.ShapeDtypeStruct((t_tot, d), F32),
                   jax.ShapeDtypeStruct((n_batch, MAX_WIN, d), F32)],
        scratch_shapes=[pltpu.VMEM((ts + MAX_WIN, d), F32)],
        compiler_params=_cparams("arbitrary"),
        name="pool_prompt",
    )(x, x, grp.mods, grp.mods, grp.mods, norm_mix, pool_w_bf, pool_scale)


def _pool_s_kernel(x_ref, st_ref, sh_ref, sc_ref, gt_ref, nrm_ref, w_ref, ps_ref,
                   xo_ref, so_ref, ext_ref, *, bb, n_new):
    x = x_ref[...]
    tm = bb * n_new
    h2 = _normmod(x, nrm_ref[...], _mod(sc_ref, tm), _mod(sh_ref, tm))
    ext_ref[:, 0:MAX_WIN, :] = st_ref[...]
    ext_ref[:, MAX_WIN:, :] = h2.reshape(bb, n_new, D_MODEL)
    outs = []
    for gi, w in enumerate(POOL_WINDOWS):
        cs = slice(gi * GROUP_CH, (gi + 1) * GROUP_CH)
        h = ext_ref[:, MAX_WIN:MAX_WIN + n_new, cs]
        acc = h
        for k in range(1, w):
            acc = acc + ext_ref[:, MAX_WIN - k:MAX_WIN - k + n_new, cs]
        pooled = (acc / float(w) - h).reshape(bb * n_new, GROUP_CH)
        outs.append(_dot(pooled.astype(BF16), w_ref[gi]))
    out = jnp.concatenate(outs, axis=-1) * ps_ref[...]
    xo_ref[...] = x + _mod(gt_ref, tm) * out
    so_ref[...] = ext_ref[:, n_new:n_new + MAX_WIN, :]


def _pool_sample(x, state_pad, grp, layer, norm_mix, pool_w_bf, pool_scale, j, n_new, bb=32):
    t_tot, d = x.shape
    n_seq = t_tot // n_new
    tm = bb * n_new
    return pl.pallas_call(
        functools.partial(_pool_s_kernel, bb=bb, n_new=n_new),
        grid=(n_seq // bb,),
        in_specs=[
            pl.BlockSpec((tm, d), lambda t: (t, 0)),
            pl.BlockSpec((bb, MAX_WIN, d), lambda t: (t, 0, 0)),
            grp.mod_spec(layer, 0, tm), grp.mod_spec(layer, 1, tm), grp.mod_spec(layer, 2, tm),
            _row_spec(layer),
            pl.BlockSpec((None, 4, GROUP_CH, GROUP_CH), lambda t: (j, 0, 0, 0)),
            _row_spec(j),
        ],
        out_specs=[
            pl.BlockSpec((tm, d), lambda t: (t, 0)),
            pl.BlockSpec((bb, MAX_WIN, d), lambda t: (t, 0, 0)),
        ],
        out_shape=[jax.ShapeDtypeStruct((t_tot, d), F32),
                   jax.ShapeDtypeStruct((n_seq, MAX_WIN, d), F32)],
        scratch_shapes=[pltpu.VMEM((bb, MAX_WIN + n_new, d), F32)],
        compiler_params=_cparams("arbitrary"),
        name="pool_sample",
    )(x, state_pad, grp.mods, grp.mods, grp.mods, norm_mix, pool_w_bf, pool_scale)


def _store_head_rows(ref, head, value):
    ref[pl.ds(head, value.shape[0], stride=N_HEADS), :] = value


def _qkv_kernel(x_ref, sh_ref, sc_ref, nrm_ref, w_ref, cos_ref, sin_ref, *rest, n_prev):
    if n_prev:
        kprev_ref, vprev_ref, rest = rest[0], rest[1], rest[2:]
    q_ref, kb_ref, vb_ref, kf_ref, vf_ref = rest
    tm = x_ref.shape[0]
    if n_prev:
        kf_ref[0:n_prev] = kprev_ref[...]
        vf_ref[0:n_prev] = vprev_ref[...]
    kf_new = kf_ref.at[n_prev]
    vf_new = vf_ref.at[n_prev]
    h = _normmod(x_ref[...], nrm_ref[...], _mod(sc_ref, tm), _mod(sh_ref, tm)).astype(BF16)
    cos = cos_ref[...]
    sin = sin_ref[...]
    lane = lax.broadcasted_iota(jnp.int32, (1, VAL_DIM), 1)
    first_half = (lane % HEAD_DIM) < (HEAD_DIM // 2)

    def rope(xh):
        rot = jnp.where(first_half, pltpu.roll(xh, VAL_DIM - HEAD_DIM // 2, 1),
                        pltpu.roll(xh, HEAD_DIM // 2, 1))
        return xh * cos + rot * sin

    pw = 2 * VAL_DIM
    for c0 in range(0, D_MODEL, pw):
        q2 = _dot(h, w_ref[:, c0:c0 + pw])
        k2 = _dot(h, w_ref[:, D_MODEL + c0:D_MODEL + c0 + pw])
        v2 = _dot(h, w_ref[:, 2 * D_MODEL + c0:2 * D_MODEL + c0 + pw])
        for u in range(2):
            cs = slice(c0 + u * VAL_DIM, c0 + (u + 1) * VAL_DIM)
            us = slice(u * VAL_DIM, (u + 1) * VAL_DIM)
            q_ref[:, cs] = (rope(q2[:, us]) * (HEAD_DIM ** -0.5)).astype(BF16)
            kr = rope(k2[:, us])
            head = (c0 + u * VAL_DIM) // VAL_DIM
            _store_head_rows(kf_new, head, kr)
            _store_head_rows(vf_new, head, v2[:, us])
            kb_ref[:, cs] = kr.astype(BF16)
        vb_ref[:, c0:c0 + pw] = v2.astype(BF16)


def _qkv(x, grp, layer, norm_mix, wqkv_bf, cos_tab, sin_tab, tm, prev_rows=None):
    t_tot, d = x.shape
    n_pos_tiles = cos_tab.shape[0] // tm
    n_prev = 0 if prev_rows is None else prev_rows[0].shape[0]
    tok = pl.BlockSpec((tm, d), lambda t: (t, 0))
    tab = pl.BlockSpec((tm, VAL_DIM), lambda t: (t % n_pos_tiles, 0))

    def rows_spec(n):
        return pl.BlockSpec((n, tm * N_HEADS, VAL_DIM), lambda t: (0, t, 0))

    rows_shape = jax.ShapeDtypeStruct((n_prev + 1, t_tot * N_HEADS, VAL_DIM), F32)
    return pl.pallas_call(
        functools.partial(_qkv_kernel, n_prev=n_prev),
        grid=(t_tot // tm,),
        in_specs=[tok, grp.mod_spec(layer, 0, tm), grp.mod_spec(layer, 1, tm), _row_spec(layer),
                  pl.BlockSpec((d, 3 * d), lambda t: (0, 0), pipeline_mode=pl.Buffered(1)), tab, tab]
        + ([rows_spec(n_prev)] * 2 if n_prev else []),
        out_specs=[tok] * 3 + [rows_spec(n_prev + 1)] * 2,
        out_shape=[jax.ShapeDtypeStruct((t_tot, d), BF16)] * 3 + [rows_shape] * 2,
        compiler_params=_cparams("arbitrary"),
        name="qkv_rope",
    )(x, grp.mods, grp.mods, norm_mix, wqkv_bf, cos_tab, sin_tab, *(prev_rows or ()))


def _rope_tables(pos):
    inv = 1.0 / (ROPE_THETA ** (jnp.arange(0, HEAD_DIM, 2, dtype=F32) / HEAD_DIM))
    ang = pos.astype(F32)[:, None] * inv[None, :]
    cos, sin = jnp.cos(ang), jnp.sin(ang)
    cos_t = jnp.concatenate([cos, cos, cos, cos], axis=-1)
    sin_t = jnp.concatenate([-sin, sin, -sin, sin], axis=-1)
    return cos_t, sin_t


def _split_components(q):
    lane = lax.broadcasted_iota(jnp.int32, (1, VAL_DIM), 1)
    zero = jnp.zeros_like(q)
    return jnp.concatenate([jnp.where(lane < HEAD_DIM, q, zero), jnp.where(lane >= HEAD_DIM, q, zero)], axis=0)


def _with_ones(v):
    return jnp.concatenate([v, jnp.ones_like(v)], axis=1)


def _online_update(s, m_old, acc_old, v1):
    chunks = [s[:, u:u + VAL_DIM] for u in range(0, s.shape[1], VAL_DIM)]
    cm = chunks[0]
    for c in chunks[1:]:
        cm = jnp.maximum(cm, c)
    m_new = jnp.maximum(m_old, jnp.max(cm, axis=-1, keepdims=True))
    alpha = jnp.exp(m_old - m_new)
    p = jnp.concatenate([jnp.exp(c - m_new) for c in chunks], axis=1).astype(BF16)
    acc_new = jnp.concatenate([alpha, alpha], axis=1) * acc_old + _dot(p, v1)
    return m_new, acc_new


def _diff_finish(acc, lam, subln, out_scale, r):
    o = acc[:, :VAL_DIM] / acc[:, VAL_DIM:]
    od = o[:r] - lam * o[r:]
    y = od * lax.rsqrt(jnp.mean(od * od, axis=-1, keepdims=True) + EPS)
    return y * subln * out_scale


KEY_BLOCKS_PER_TRIP = 4

def _attn_p_kernel(lam_ref, q_ref, k_ref, v_ref, sub_ref, o_ref, qs_ref, m_ref, acc_ref, *, tq, rc, out_scale):
    qi = pl.program_id(2)
    r = 2 * tq
    qs_ref[...] = _split_components(q_ref[...])
    m_ref[...] = jnp.full(m_ref.shape, -jnp.inf, F32)
    acc_ref[...] = jnp.zeros(acc_ref.shape, F32)

    def step(kj, masked):
        off = pl.multiple_of(kj * tq, tq)
        k = k_ref[pl.ds(off, tq), :]
        v1 = _with_ones(v_ref[pl.ds(off, tq), :])
        for r0 in range(0, r, rc):
            rows = slice(r0, r0 + rc)
            nk = (r0 % tq) + rc if masked else tq
            s = _dot_nt(qs_ref[rows, :], k[:nk])
            if masked:
                row = (r0 + lax.broadcasted_iota(jnp.int32, (rc, 1), 0)) % tq
                col = lax.broadcasted_iota(jnp.int32, (1, nk), 1)
                s = jnp.where(col <= row, s, -jnp.inf)
            m_new, acc_new = _online_update(s, m_ref[rows, :], acc_ref[rows, :], v1[:nk])
            m_ref[rows, :] = m_new
            acc_ref[rows, :] = acc_new

    def body(t, carry):
        for u in range(KEY_BLOCKS_PER_TRIP):
            step(KEY_BLOCKS_PER_TRIP * t + u, False)
        return carry

    n_trips = qi // KEY_BLOCKS_PER_TRIP
    lax.fori_loop(0, n_trips, body, 0)
    left = qi - n_trips * KEY_BLOCKS_PER_TRIP

    @pl.when(left >= 2)
    def _():
        step(n_trips * KEY_BLOCKS_PER_TRIP, False)
        step(n_trips * KEY_BLOCKS_PER_TRIP + 1, False)

    @pl.when(left % 2 == 1)
    def _():
        step(qi - 1, False)
        step(qi, True)

    @pl.when(left % 2 == 0)
    def _():
        step(qi, True)
    o_ref[...] = _diff_finish(acc_ref[...], lam_ref[0], sub_ref[...], out_scale, tq).astype(BF16)


def _attn_prompt(q, k, v, lam, subln_row, n_batch, seq, out_scale, tq=512, rc=256):
    t_tot, d = q.shape
    nq = seq // tq
    return pl.pallas_call(
        functools.partial(_attn_p_kernel, tq=tq, rc=rc, out_scale=out_scale),
        grid=(n_batch, N_HEADS, nq),
        in_specs=[
            pl.BlockSpec(memory_space=pltpu.SMEM),
            pl.BlockSpec((tq, VAL_DIM), lambda b, h, i: (b * nq + i, h)),
            pl.BlockSpec((seq, VAL_DIM), lambda b, h, i: (b, h)),
            pl.BlockSpec((seq, VAL_DIM), lambda b, h, i: (b, h)),
            pl.BlockSpec((1, VAL_DIM), lambda b, h, i: (0, 0)),
        ],
        out_specs=pl.BlockSpec((tq, VAL_DIM), lambda b, h, i: (b * nq + i, h)),
        out_shape=jax.ShapeDtypeStruct((t_tot, d), BF16),
        scratch_shapes=[pltpu.VMEM((2 * tq, VAL_DIM), BF16), pltpu.VMEM((2 * tq, VAL_DIM), F32),
                        pltpu.VMEM((2 * tq, 2 * VAL_DIM), F32)],
        compiler_params=_cparams("arbitrary", "arbitrary", "arbitrary"),
        name="attn_prompt",
    )(lam, q, k, v, subln_row)


def _attn_s_kernel(pt_ref, lam_ref, q_ref, kn_ref, vn_ref, sub_ref, *rest, n_pages, n_new, out_scale):
    del pt_ref
    k_refs = rest[:n_pages]
    v_refs = rest[n_pages:2 * n_pages]
    o_ref = rest[2 * n_pages]
    r = n_new * N_HEADS
    qs = _split_components(q_ref[...])
    row = lax.broadcasted_iota(jnp.int32, (2 * r, 1), 0)
    row_h = row % N_HEADS
    row_t = (row // N_HEADS) % n_new
    kcols = PAGE_SIZE * N_HEADS
    col_h = lax.broadcasted_iota(jnp.int32, (1, kcols), 1) % N_HEADS
    same_head = row_h == col_h

    m = jnp.full((2 * r, VAL_DIM), -jnp.inf, F32)
    acc = jnp.zeros((2 * r, 2 * VAL_DIM), F32)
    for p_i in range(n_pages):
        k = k_refs[p_i][...].reshape(kcols, VAL_DIM).astype(BF16)
        v = v_refs[p_i][...].reshape(kcols, VAL_DIM).astype(BF16)
        s = jnp.where(same_head, _dot_nt(qs, k), -jnp.inf)
        m, acc = _online_update(s, m, acc, _with_ones(v))

    ncol = lax.broadcasted_iota(jnp.int32, (1, VAL_DIM), 1)
    new_ok = (ncol < r) & (row_h == ncol % N_HEADS) & (ncol // N_HEADS <= row_t)
    pad = jnp.zeros((VAL_DIM - r, VAL_DIM), BF16)
    kn = jnp.concatenate([kn_ref[...], pad], axis=0)
    vn = jnp.concatenate([vn_ref[...], pad], axis=0)
    s = jnp.where(new_ok, _dot_nt(qs, kn), -jnp.inf)
    m, acc = _online_update(s, m, acc, _with_ones(vn))
    o_ref[...] = _diff_finish(acc, lam_ref[0], sub_ref[...], out_scale, r).astype(BF16)


def _attn_sample(q3, kn3, vn3, cache_k, cache_v, page_table, lam, subln_row, j, out_scale):
    n_seq, r, _ = q3.shape
    n_pages = page_table.shape[1]
    n_new = r // N_HEADS
    row3 = pl.BlockSpec((None, r, VAL_DIM), lambda b, pt: (b, 0, 0))

    def page_spec(p):
        return pl.BlockSpec((None, None, PAGE_SIZE, N_HEADS, VAL_DIM), lambda b, pt: (j, pt[b, p], 0, 0, 0))

    grid_spec = pltpu.PrefetchScalarGridSpec(
        num_scalar_prefetch=1,
        grid=(n_seq,),
        in_specs=[pl.BlockSpec(memory_space=pltpu.SMEM), row3, row3, row3,
                  pl.BlockSpec((1, VAL_DIM), lambda b, pt: (0, 0))]
        + [page_spec(p) for p in range(n_pages)] * 2,
        out_specs=row3,
    )
    return pl.pallas_call(
        functools.partial(_attn_s_kernel, n_pages=n_pages, n_new=n_new, out_scale=out_scale),
        grid_spec=grid_spec,
        out_shape=jax.ShapeDtypeStruct((n_seq, r, VAL_DIM), BF16),
        compiler_params=_cparams("arbitrary"),
        name="attn_sample",
    )(page_table, lam, q3, kn3, vn3, subln_row, *([cache_k] * n_pages), *([cache_v] * n_pages))


def _wo_kernel(x_ref, o_ref, gt_ref, w_ref, xo_ref):
    xo_ref[...] = x_ref[...] + _mod(gt_ref, x_ref.shape[0]) * _dot(o_ref[...], w_ref[...])


def _attn_out(x, o, grp, layer, wo_bf, tm):
    t_tot, d = x.shape
    tok = pl.BlockSpec((tm, d), lambda t: (t, 0))
    return pl.pallas_call(
        _wo_kernel,
        grid=(t_tot // tm,),
        in_specs=[tok, tok, grp.mod_spec(layer, 2, tm), pl.BlockSpec((d, d), lambda t: (0, 0))],
        out_specs=tok,
        out_shape=jax.ShapeDtypeStruct((t_tot, d), F32),
        compiler_params=_cparams("arbitrary"),
        name="attn_out",
    )(x, o, grp.mods, wo_bf)


def _route(logits_t, bias):
    s_all = jax.nn.sigmoid(logits_t)
    sb_all = s_all + bias
    s = [s_all[e:e + 1, :] for e in range(N_EXPERTS)]
    sb = [sb_all[e:e + 1, :] for e in range(N_EXPERTS)]
    n = EXPERTS_PER_GROUP
    gscore = []
    for g in range(N_EXPERT_GROUPS):
        v = sb[g * n:(g + 1) * n]
        best = v[0] + v[1]
        for a in range(n):
            for b in range(a + 1, n):
                if (a, b) != (0, 1):
                    best = jnp.maximum(best, v[a] + v[b])
        gscore.append(best)
    gsel = jnp.zeros_like(gscore[0], dtype=jnp.int32)
    gbest = gscore[0]
    for g in range(1, N_EXPERT_GROUPS):
        better = gscore[g] > gbest
        gsel = jnp.where(better, g, gsel)
        gbest = jnp.where(better, gscore[g], gbest)

    def pick(rows, r):
        out = rows[r]
        for g in range(1, N_EXPERT_GROUPS):
            out = jnp.where(gsel == g, rows[g * n + r], out)
        return out

    vb = [pick(sb, r) for r in range(n)]
    vu = [pick(s, r) for r in range(n)]

    def argmax_first(vals):
        idx = jnp.zeros_like(gsel)
        best = vals[0]
        for r in range(1, n):
            better = vals[r] > best
            idx = jnp.where(better, r, idx)
            best = jnp.where(better, vals[r], best)
        return idx

    i1 = argmax_first(vb)
    i2 = argmax_first([jnp.where(i1 == r, -jnp.inf, vb[r]) for r in range(n)])

    def take(vals, idx):
        out = vals[0]
        for r in range(1, n):
            out = jnp.where(idx == r, vals[r], out)
        return out

    w1 = take(vu, i1)
    w2 = take(vu, i2)
    tot = w1 + w2
    e1 = gsel * n + i1
    e2 = gsel * n + i2
    erow = lax.broadcasted_iota(jnp.int32, logits_t.shape, 0)
    return jnp.where(erow == e1, w1 / tot, 0.0) + jnp.where(erow == e2, w2 / tot, 0.0)


def _router_logits(h2, h2_bf, rw):
    rw_hi = rw.astype(BF16)
    rw_lo = (rw - rw_hi.astype(F32)).astype(BF16)
    h_lo = (h2 - h2_bf.astype(F32)).astype(BF16)
    return _dot_nt(rw_hi, h2_bf) + (_dot_nt(rw_hi, h_lo) + _dot_nt(rw_lo, h2_bf))


def _moe_dense_kernel(x_ref, sh_ref, sc_ref, gt_ref, nrm_ref, rw_ref, rb_ref, wg_ref, wu_ref, wd_ref, *rest,
                      tm, n_e, final):
    fg_ref = rest[0] if final else None
    xo_ref, h_ref, gates_ref, acc_ref = rest[1:] if final else rest
    e = pl.program_id(1)

    @pl.when(e == 0)
    def _():
        h2 = _normmod(x_ref[...], nrm_ref[...], _mod(sc_ref, tm), _mod(sh_ref, tm))
        h2_bf = h2.astype(BF16)
        h_ref[...] = h2_bf
        gates_t = _route(_router_logits(h2, h2_bf, rw_ref[...]), rb_ref[...])
        pad = jnp.zeros((128 - N_EXPERTS, tm), F32)
        gates_ref[...] = jnp.concatenate([gates_t, pad], axis=0).T
        acc_ref[...] = jnp.zeros(acc_ref.shape, F32)

    h = h_ref[...]
    gates = gates_ref[...]
    lane = lax.broadcasted_iota(jnp.int32, (1, 128), 1)
    acts = []
    for u in range(n_e):
        hg = _dot(h, wg_ref[u].astype(BF16))
        hu = _dot(h, wu_ref[u].astype(BF16))
        gate = jnp.sum(jnp.where(lane == e * n_e + u, gates, 0.0), axis=-1, keepdims=True)
        acts.append(((hg * jax.nn.sigmoid(hg)) * hu * gate).astype(BF16))
    act = jnp.concatenate(acts, axis=1)
    acc_ref[...] += _dot(act, wd_ref[...].reshape(n_e * D_EXPERT, D_MODEL).astype(BF16))

    @pl.when(e == N_EXPERTS // n_e - 1)
    def _():
        xo = x_ref[...] + _mod(gt_ref, tm) * acc_ref[...]
        if final:
            xo = (xo * lax.rsqrt(jnp.mean(xo * xo, axis=-1, keepdims=True) + EPS)) * fg_ref[...]
        xo_ref[...] = xo


def _moe_dense(x, grp, layer, norm_ffn, router_wt, router_b, wg, wu, wd, tm, final_gain=None, n_e=2):
    t_tot, d = x.shape
    tok = pl.BlockSpec((tm, d), lambda t, e: (t, 0))
    final = final_gain is not None
    extra_specs, extra_args = ([pl.BlockSpec((1, d), lambda t, e: (0, 0))], [final_gain]) if final else ([], [])
    return pl.pallas_call(
        functools.partial(_moe_dense_kernel, tm=tm, n_e=n_e, final=final),
        grid=(t_tot // tm, N_EXPERTS // n_e),
        in_specs=[tok, grp.mod_spec(layer, 3, tm), grp.mod_spec(layer, 4, tm), grp.mod_spec(layer, 5, tm),
                  _row_spec(layer),
                  pl.BlockSpec((N_EXPERTS, d), lambda t, e: (0, 0)),
                  pl.BlockSpec((N_EXPERTS, 1), lambda t, e: (0, 0)),
                  pl.BlockSpec((None, n_e, d, D_EXPERT), lambda t, e: (layer, e, 0, 0)),
                  pl.BlockSpec((None, n_e, d, D_EXPERT), lambda t, e: (layer, e, 0, 0)),
                  pl.BlockSpec((None, n_e, D_EXPERT, d), lambda t, e: (layer, e, 0, 0))] + extra_specs,
        out_specs=tok,
        out_shape=jax.ShapeDtypeStruct((t_tot, d), F32),
        scratch_shapes=[pltpu.VMEM((tm, d), BF16), pltpu.VMEM((tm, 128), F32),
                        pltpu.VMEM((tm, d), F32)],
        compiler_params=_cparams("arbitrary", "arbitrary"),
        name="moe_dense",
    )(x, grp.mods, grp.mods, grp.mods, norm_ffn, router_wt, router_b, wg, wu, wd, *extra_args)


def kernel(x_prompt, x_sample, c_prompt, c_sample, state_pool, cache_k, cache_v, page_table, ada_w, ada_b, norm_mix, norm_ffn, norm_final, pool_w, pool_scale, attn_w_qkv, attn_w_o, lambda_q1, lambda_k1, lambda_q2, lambda_k2, subln, router_w, router_bias, moe_w_gate, moe_w_up, moe_w_down):
    bp, n_p, d = x_prompt.shape
    bs, n_s, _ = x_sample.shape
    depth = ada_w.shape[0]
    past_len = page_table.shape[1] * PAGE_SIZE
    tm_p, tm_s = 512, 256

    mods_s, mods_p = _ada_mods(c_sample, c_prompt, ada_w, ada_b)
    grp_p = _Group(mods_p.reshape(depth, bp, 1, N_MODS * d), many_per_tile=False, seq=n_p)
    grp_s = _Group(mods_s, many_per_tile=True, seq=n_s)

    xp = x_prompt.reshape(bp * n_p, d)
    xs = x_sample.reshape(bs * n_s, d)

    norm_mix3 = norm_mix.reshape(depth, 1, d)
    norm_ffn3 = norm_ffn.reshape(depth, 1, d)
    pool_w_bf = pool_w.astype(BF16)
    pool_scale3 = pool_scale.reshape(-1, 1, d)
    wqkv_bf = attn_w_qkv.astype(BF16)
    wo_bf = attn_w_o.astype(BF16)
    router_wt = router_w.T
    router_b = router_bias.reshape(N_EXPERTS, 1)

    cos_p, sin_p = _rope_tables(jnp.arange(n_p))
    cos_s, sin_s = _rope_tables(past_len + jnp.arange(n_s))
    cos_s = jnp.tile(cos_s, (tm_s // n_s, 1))
    sin_s = jnp.tile(sin_s, (tm_s // n_s, 1))

    pool_p, pool_s = [], []
    rows_p = rows_s = None
    g_row = norm_final.reshape(1, d)
    for i in range(depth):
        j = i // 2
        if i % 2 == 0:
            xp, st_p = _pool_prompt(xp, grp_p, i, norm_mix3, pool_w_bf, pool_scale3, j, bp, ts=tm_p)
            state_pad = jnp.pad(state_pool[j], ((0, 0), (1, 0), (0, 0)))
            xs, st_s = _pool_sample(xs, state_pad, grp_s, i, norm_mix3, pool_w_bf, pool_scale3, j, n_s)
            pool_p.append(st_p[:, 1:])
            pool_s.append(st_s[:, 1:])
        else:
            lam_init = 0.8 - 0.6 * math.exp(-0.3 * i)
            lam = (jnp.exp(jnp.sum(lambda_q1[j] * lambda_k1[j])) - jnp.exp(jnp.sum(lambda_q2[j] * lambda_k2[j]))
                   + lam_init).reshape(1).astype(F32)
            sub_row = subln[j].reshape(1, VAL_DIM)
            out_scale = 1.0 - lam_init

            q, kb, vb, *rows_p = _qkv(xp, grp_p, i, norm_mix3, wqkv_bf[j], cos_p, sin_p, tm_p, rows_p)
            o = _attn_prompt(q, kb, vb, lam, sub_row, bp, n_p, out_scale)
            xp = _attn_out(xp, o, grp_p, i, wo_bf[j], tm_p)

            q, kb, vb, *rows_s = _qkv(xs, grp_s, i, norm_mix3, wqkv_bf[j], cos_s, sin_s, tm_s, rows_s)
            r = n_s * N_HEADS
            o3 = _attn_sample(q.reshape(bs, r, VAL_DIM), kb.reshape(bs, r, VAL_DIM), vb.reshape(bs, r, VAL_DIM),
                              cache_k, cache_v, page_table, lam, sub_row, j, out_scale)
            xs = _attn_out(xs, o3.reshape(bs * n_s, d), grp_s, i, wo_bf[j], tm_s)

        final_gain = g_row if i == depth - 1 else None
        xp = _moe_dense(xp, grp_p, i, norm_ffn3, router_wt, router_b, moe_w_gate, moe_w_up, moe_w_down,
                        2 * tm_p, final_gain)
        xs = _moe_dense(xs, grp_s, i, norm_ffn3, router_wt, router_b, moe_w_gate, moe_w_up, moe_w_down,
                        min(2 * tm_p, bs * n_s), final_gain)

    n_attn = depth // 2
    return (xp.reshape(bp, n_p, d), xs.reshape(bs, n_s, d), jnp.stack(pool_p), jnp.stack(pool_s),
            rows_p[0].reshape(n_attn, bp, n_p, N_HEADS, VAL_DIM), rows_p[1].reshape(n_attn, bp, n_p, N_HEADS, VAL_DIM),
            rows_s[0].reshape(n_attn, bs, n_s, N_HEADS, VAL_DIM), rows_s[1].reshape(n_attn, bs, n_s, N_HEADS, VAL_DIM))
```

```python
import functools
import math

import jax
import jax.numpy as jnp
from jax import lax
from jax.experimental import pallas as pl
from jax.experimental.pallas import tpu as pltpu

F32 = jnp.float32
BF16 = jnp.bfloat16

D_MODEL = 1024
N_HEADS = 8
HEAD_DIM = 64
VAL_DIM = 128
ROPE_THETA = 10000.0
POOL_WINDOWS = (2, 4, 8, 16)
GROUP_CH = 256
MAX_WIN = 16
N_EXPERTS = 16
EXPERTS_PER_GROUP = 4
N_EXPERT_GROUPS = 4
D_EXPERT = 256
PAGE_SIZE = 128
EPS = 1e-6
N_MODS = 6

VMEM_LIMIT_BYTES = 52 * 1024 * 1024


def _cparams(*sem):
    return pltpu.CompilerParams(dimension_semantics=sem, vmem_limit_bytes=VMEM_LIMIT_BYTES)


def _normmod(x, g, sc, sh):
    r = lax.rsqrt(jnp.mean(x * x, axis=-1, keepdims=True) + EPS)
    return (x * r) * g * (1.0 + sc) + sh


def _dot(a, b):
    return jnp.dot(a, b, preferred_element_type=F32)


def _dot_nt(a, b):
    return lax.dot_general(a, b, (((1,), (1,)), ((), ())), preferred_element_type=F32)


def _ada_kernel(c_ref, w_ref, b_ref, os_ref, op_ref, *, n_s):
    c = c_ref[...]
    a = (c * jax.nn.sigmoid(c)).astype(BF16)
    r = _dot(a, w_ref[...].astype(BF16)) + b_ref[...]
    os_ref[...] = r[:n_s]
    op_ref[...] = r[n_s:]


def _ada_mods(c_sample, c_prompt, ada_w, ada_b, tn=1536):
    depth, d, n = ada_w.shape
    n_s, n_p = c_sample.shape[0], c_prompt.shape[0]
    c_all = jnp.concatenate([c_sample, c_prompt], axis=0)
    return pl.pallas_call(
        functools.partial(_ada_kernel, n_s=n_s),
        grid=(depth, n // tn),
        in_specs=[
            pl.BlockSpec((n_s + n_p, d), lambda i, j: (0, 0)),
            pl.BlockSpec((None, d, tn), lambda i, j: (i, 0, j)),
            pl.BlockSpec((None, 1, tn), lambda i, j: (i, 0, j)),
        ],
        out_specs=[
            pl.BlockSpec((None, n_s, tn), lambda i, j: (i, 0, j)),
            pl.BlockSpec((None, n_p, tn), lambda i, j: (i, 0, j)),
        ],
        out_shape=[jax.ShapeDtypeStruct((depth, n_s, n), F32),
                   jax.ShapeDtypeStruct((depth, n_p, n), F32)],
        compiler_params=_cparams("arbitrary", "arbitrary"),
        name="ada_mods",
    )(c_all, ada_w, ada_b.reshape(depth, 1, n))


class _Group:
    def __init__(self, mods, many_per_tile, seq):
        self.mods = mods
        self.many_per_tile = many_per_tile
        self.seq = seq

    def mod_spec(self, layer, chunk, tm):
        if self.many_per_tile:
            return pl.BlockSpec((None, tm // self.seq, D_MODEL), lambda t, *_: (layer, t, chunk))
        tpb = self.seq // tm
        return pl.BlockSpec((None, None, 1, D_MODEL), lambda t, *_: (layer, t // tpb, 0, chunk))


def _mod(ref, tm):
    m = ref[...]
    n, d = m.shape
    if n == 1 or n == tm:
        return m
    return jnp.broadcast_to(m[:, None, :], (n, tm // n, d)).reshape(tm, d)


def _row_spec(layer, width=D_MODEL):
    return pl.BlockSpec((None, 1, width), lambda t, *_: (layer, 0, 0))


def _pool_p_kernel(x_ref, halo_ref, sh_ref, sc_ref, gt_ref, nrm_ref, w_ref, ps_ref,
                   xo_ref, st_ref, ext_ref, *, ts, tpb, rc):
    s = pl.program_id(0) % tpb
    g = nrm_ref[...]
    sc = sc_ref[...]
    sh = sh_ref[...]
    hh = _normmod(halo_ref[...], g, sc, sh)
    ext_ref[0:MAX_WIN, :] = jnp.where(s == 0, 0.0, hh)
    ext_ref[MAX_WIN:, :] = _normmod(x_ref[...], g, sc, sh)
    for r0 in range(0, ts, rc):
        pos = s * ts + r0 + lax.broadcasted_iota(jnp.int32, (rc, 1), 0)
        for gi, w in enumerate(POOL_WINDOWS):
            cs = slice(gi * GROUP_CH, (gi + 1) * GROUP_CH)
            acc = ext_ref[r0:r0 + MAX_WIN + rc, cs]
            h = acc[MAX_WIN:]
            span = 1
            while span < w:
                acc = acc + pltpu.roll(acc, span, 0)
                span *= 2
            cnt = jnp.minimum(w, pos + 1).astype(F32)
            pooled = acc[MAX_WIN:] / cnt - h
            o = _dot(pooled.astype(BF16), w_ref[gi]) * ps_ref[:, cs]
            xo_ref[r0:r0 + rc, cs] = x_ref[r0:r0 + rc, cs] + gt_ref[:, cs] * o

    @pl.when(s == tpb - 1)
    def _():
        st_ref[...] = ext_ref[ts:ts + MAX_WIN, :]


def _pool_prompt(x, grp, layer, norm_mix, pool_w_bf, pool_scale, j, n_batch, ts=512, rc=128):
    t_tot, d = x.shape
    tpb = grp.seq // ts
    hb = ts // MAX_WIN
    return pl.pallas_call(
        functools.partial(_pool_p_kernel, ts=ts, tpb=tpb, rc=rc),
        grid=(t_tot // ts,),
        in_specs=[
            pl.BlockSpec((ts, d), lambda t: (t, 0)),
            pl.BlockSpec((MAX_WIN, d), lambda t: (jnp.maximum(t * hb - 1, 0), 0)),
            grp.mod_spec(layer, 0, ts), grp.mod_spec(layer, 1, ts), grp.mod_spec(layer, 2, ts),
            _row_spec(layer),
            pl.BlockSpec((None, 4, GROUP_CH, GROUP_CH), lambda t: (j, 0, 0, 0)),
            _row_spec(j),
        ],
        out_specs=[
            pl.BlockSpec((ts, d), lambda t: (t, 0)),
            pl.BlockSpec((None, MAX_WIN, d), lambda t: (t // tpb, 0, 0)),
        ],
        out_shape=[jax.ShapeDtypeStruct((t_tot, d), F32),
                   jax.ShapeDtypeStruct((n_batch, MAX_WIN, d), F32)],
        scratch_shapes=[pltpu.VMEM((ts + MAX_WIN, d), F32)],
        compiler_params=_cparams("arbitrary"),
        name="pool_prompt",
    )(x, x, grp.mods, grp.mods, grp.mods, norm_mix, pool_w_bf, pool_scale)


def _pool_s_kernel(x_ref, st_ref, sh_ref, sc_ref, gt_ref, nrm_ref, w_ref, ps_ref,
                   xo_ref, so_ref, ext_ref, *, bb, n_new):
    x = x_ref[...]
    tm = bb * n_new
    h2 = _normmod(x, nrm_ref[...], _mod(sc_ref, tm), _mod(sh_ref, tm))
    ext_ref[:, 0:MAX_WIN, :] = st_ref[...]
    ext_ref[:, MAX_WIN:, :] = h2.reshape(bb, n_new, D_MODEL)
    outs = []
    for gi, w in enumerate(POOL_WINDOWS):
        cs = slice(gi * GROUP_CH, (gi + 1) * GROUP_CH)
        h = ext_ref[:, MAX_WIN:MAX_WIN + n_new, cs]
        acc = h
        for k in range(1, w):
            acc = acc + ext_ref[:, MAX_WIN - k:MAX_WIN - k + n_new, cs]
        pooled = (acc / float(w) - h).reshape(bb * n_new, GROUP_CH)
        outs.append(_dot(pooled.astype(BF16), w_ref[gi]))
    out = jnp.concatenate(outs, axis=-1) * ps_ref[...]
    xo_ref[...] = x + _mod(gt_ref, tm) * out
    so_ref[...] = ext_ref[:, n_new:n_new + MAX_WIN, :]


def _pool_sample(x, state_pad, grp, layer, norm_mix, pool_w_bf, pool_scale, j, n_new, bb=32):
    t_tot, d = x.shape
    n_seq = t_tot // n_new
    tm = bb * n_new
    return pl.pallas_call(
        functools.partial(_pool_s_kernel, bb=bb, n_new=n_new),
        grid=(n_seq // bb,),
        in_specs=[
            pl.BlockSpec((tm, d), lambda t: (t, 0)),
            pl.BlockSpec((bb, MAX_WIN, d), lambda t: (t, 0, 0)),
            grp.mod_spec(layer, 0, tm), grp.mod_spec(layer, 1, tm), grp.mod_spec(layer, 2, tm),
            _row_spec(layer),
            pl.BlockSpec((None, 4, GROUP_CH, GROUP_CH), lambda t: (j, 0, 0, 0)),
            _row_spec(j),
        ],
        out_specs=[
            pl.BlockSpec((tm, d), lambda t: (t, 0)),
            pl.BlockSpec((bb, MAX_WIN, d), lambda t: (t, 0, 0)),
        ],
        out_shape=[jax.ShapeDtypeStruct((t_tot, d), F32),
                   jax.ShapeDtypeStruct((n_seq, MAX_WIN, d), F32)],
        scratch_shapes=[pltpu.VMEM((bb, MAX_WIN + n_new, d), F32)],
        compiler_params=_cparams("arbitrary"),
        name="pool_sample",
    )(x, state_pad, grp.mods, grp.mods, grp.mods, norm_mix, pool_w_bf, pool_scale)


def _store_head_rows(ref, head, value):
    ref[pl.ds(head, value.shape[0], stride=N_HEADS), :] = value


def _qkv_kernel(x_ref, sh_ref, sc_ref, nrm_ref, w_ref, cos_ref, sin_ref, *rest, n_prev):
    if n_prev:
        kprev_ref, vprev_ref, rest = rest[0], rest[1], rest[2:]
    q_ref, kb_ref, vb_ref, kf_ref, vf_ref = rest
    tm = x_ref.shape[0]
    if n_prev:
        kf_ref[0:n_prev] = kprev_ref[...]
        vf_ref[0:n_prev] = vprev_ref[...]
    kf_new = kf_ref.at[n_prev]
    vf_new = vf_ref.at[n_prev]
    h = _normmod(x_ref[...], nrm_ref[...], _mod(sc_ref, tm), _mod(sh_ref, tm)).astype(BF16)
    cos = cos_ref[...]
    sin = sin_ref[...]
    lane = lax.broadcasted_iota(jnp.int32, (1, VAL_DIM), 1)
    first_half = (lane % HEAD_DIM) < (HEAD_DIM // 2)

    def rope(xh):
        rot = jnp.where(first_half, pltpu.roll(xh, VAL_DIM - HEAD_DIM // 2, 1),
                        pltpu.roll(xh, HEAD_DIM // 2, 1))
        return xh * cos + rot * sin

    pw = 2 * VAL_DIM
    for c0 in range(0, D_MODEL, pw):
        q2 = _dot(h, w_ref[:, c0:c0 + pw])
        k2 = _dot(h, w_ref[:, D_MODEL + c0:D_MODEL + c0 + pw])
        v2 = _dot(h, w_ref[:, 2 * D_MODEL + c0:2 * D_MODEL + c0 + pw])
        for u in range(2):
            cs = slice(c0 + u * VAL_DIM, c0 + (u + 1) * VAL_DIM)
            us = slice(u * VAL_DIM, (u + 1) * VAL_DIM)
            q_ref[:, cs] = (rope(q2[:, us]) * (HEAD_DIM ** -0.5)).astype(BF16)
            kr = rope(k2[:, us])
            head = (c0 + u * VAL_DIM) // VAL_DIM
            _store_head_rows(kf_new, head, kr)
            _store_head_rows(vf_new, head, v2[:, us])
            kb_ref[:, cs] = kr.astype(BF16)
        vb_ref[:, c0:c0 + pw] = v2.astype(BF16)


def _qkv(x, grp, layer, norm_mix, wqkv_bf, cos_tab, sin_tab, tm, prev_rows=None):
    t_tot, d = x.shape
    n_pos_tiles = cos_tab.shape[0] // tm
    n_prev = 0 if prev_rows is None else prev_rows[0].shape[0]
    tok = pl.BlockSpec((tm, d), lambda t: (t, 0))
    tab = pl.BlockSpec((tm, VAL_DIM), lambda t: (t % n_pos_tiles, 0))

    def rows_spec(n):
        return pl.BlockSpec((n, tm * N_HEADS, VAL_DIM), lambda t: (0, t, 0))

    rows_shape = jax.ShapeDtypeStruct((n_prev + 1, t_tot * N_HEADS, VAL_DIM), F32)
    return pl.pallas_call(
        functools.partial(_qkv_kernel, n_prev=n_prev),
        grid=(t_tot // tm,),
        in_specs=[tok, grp.mod_spec(layer, 0, tm), grp.mod_spec(layer, 1, tm), _row_spec(layer),
                  pl.BlockSpec((d, 3 * d), lambda t: (0, 0), pipeline_mode=pl.Buffered(1)), tab, tab]
        + ([rows_spec(n_prev)] * 2 if n_prev else []),
        out_specs=[tok] * 3 + [rows_spec(n_prev + 1)] * 2,
        out_shape=[jax.ShapeDtypeStruct((t_tot, d), BF16)] * 3 + [rows_shape] * 2,
        compiler_params=_cparams("arbitrary"),
        name="qkv_rope",
    )(x, grp.mods, grp.mods, norm_mix, wqkv_bf, cos_tab, sin_tab, *(prev_rows or ()))


def _rope_tables(pos):
    inv = 1.0 / (ROPE_THETA ** (jnp.arange(0, HEAD_DIM, 2, dtype=F32) / HEAD_DIM))
    ang = pos.astype(F32)[:, None] * inv[None, :]
    cos, sin = jnp.cos(ang), jnp.sin(ang)
    cos_t = jnp.concatenate([cos, cos, cos, cos], axis=-1)
    sin_t = jnp.concatenate([-sin, sin, -sin, sin], axis=-1)
    return cos_t, sin_t


def _split_components(q):
    lane = lax.broadcasted_iota(jnp.int32, (1, VAL_DIM), 1)
    zero = jnp.zeros_like(q)
    return jnp.concatenate([jnp.where(lane < HEAD_DIM, q, zero), jnp.where(lane >= HEAD_DIM, q, zero)], axis=0)


def _with_ones(v):
    return jnp.concatenate([v, jnp.ones_like(v)], axis=1)


def _online_update(s, m_old, acc_old, v1):
    chunks = [s[:, u:u + VAL_DIM] for u in range(0, s.shape[1], VAL_DIM)]
    cm = chunks[0]
    for c in chunks[1:]:
        cm = jnp.maximum(cm, c)
    m_new = jnp.maximum(m_old, jnp.max(cm, axis=-1, keepdims=True))
    alpha = jnp.exp(m_old - m_new)
    p = jnp.concatenate([jnp.exp(c - m_new) for c in chunks], axis=1).astype(BF16)
    acc_new = jnp.concatenate([alpha, alpha], axis=1) * acc_old + _dot(p, v1)
    return m_new, acc_new


def _diff_finish(acc, lam, subln, out_scale, r):
    o = acc[:, :VAL_DIM] / acc[:, VAL_DIM:]
    od = o[:r] - lam * o[r:]
    y = od * lax.rsqrt(jnp.mean(od * od, axis=-1, keepdims=True) + EPS)
    return y * subln * out_scale


KEY_BLOCKS_PER_TRIP = 2

def _attn_p_kernel(lam_ref, q_ref, k_ref, v_ref, sub_ref, o_ref, qs_ref, m_ref, acc_ref, *, tq, rc, hp, out_scale):
    qi = pl.program_id(2)
    r = 2 * tq
    heads = [slice(hh * VAL_DIM, (hh + 1) * VAL_DIM) for hh in range(hp)]
    for hh, hs in enumerate(heads):
        qs_ref[hh] = _split_components(q_ref[:, hs])
    m_ref[...] = jnp.full(m_ref.shape, -jnp.inf, F32)
    acc_ref[...] = jnp.zeros(acc_ref.shape, F32)

    def step(kj, masked):
        off = pl.multiple_of(kj * tq, tq)
        for hh, hs in enumerate(heads):
            k = k_ref[pl.ds(off, tq), hs]
            v1 = _with_ones(v_ref[pl.ds(off, tq), hs])
            for r0 in range(0, r, rc):
                rows = slice(r0, r0 + rc)
                nk = (r0 % tq) + rc if masked else tq
                s = _dot_nt(qs_ref[hh, rows, :], k[:nk])
                if masked:
                    row = (r0 + lax.broadcasted_iota(jnp.int32, (rc, 1), 0)) % tq
                    col = lax.broadcasted_iota(jnp.int32, (1, nk), 1)
                    s = jnp.where(col <= row, s, -jnp.inf)
                m_new, acc_new = _online_update(s, m_ref[hh, rows, :], acc_ref[hh, rows, :], v1[:nk])
                m_ref[hh, rows, :] = m_new
                acc_ref[hh, rows, :] = acc_new

    def body(t, carry):
        for u in range(KEY_BLOCKS_PER_TRIP):
            step(KEY_BLOCKS_PER_TRIP * t + u, False)
        return carry

    n_trips = qi // KEY_BLOCKS_PER_TRIP
    lax.fori_loop(0, n_trips, body, 0)

    @pl.when(qi - n_trips * KEY_BLOCKS_PER_TRIP == 1)
    def _():
        step(qi - 1, False)

    step(qi, True)
    for hh, hs in enumerate(heads):
        o_ref[:, hs] = _diff_finish(acc_ref[hh], lam_ref[0], sub_ref[...], out_scale, tq).astype(BF16)


def _attn_prompt(q, k, v, lam, subln_row, n_batch, seq, out_scale, tq=512, rc=256, hp=2):
    t_tot, d = q.shape
    nq = seq // tq
    return pl.pallas_call(
        functools.partial(_attn_p_kernel, tq=tq, rc=rc, hp=hp, out_scale=out_scale),
        grid=(n_batch, N_HEADS // hp, nq),
        in_specs=[
            pl.BlockSpec(memory_space=pltpu.SMEM),
            pl.BlockSpec((tq, hp * VAL_DIM), lambda b, h, i: (b * nq + i, h)),
            pl.BlockSpec((seq, hp * VAL_DIM), lambda b, h, i: (b, h)),
            pl.BlockSpec((seq, hp * VAL_DIM), lambda b, h, i: (b, h)),
            pl.BlockSpec((1, VAL_DIM), lambda b, h, i: (0, 0)),
        ],
        out_specs=pl.BlockSpec((tq, hp * VAL_DIM), lambda b, h, i: (b * nq + i, h)),
        out_shape=jax.ShapeDtypeStruct((t_tot, d), BF16),
        scratch_shapes=[pltpu.VMEM((hp, 2 * tq, VAL_DIM), BF16), pltpu.VMEM((hp, 2 * tq, VAL_DIM), F32),
                        pltpu.VMEM((hp, 2 * tq, 2 * VAL_DIM), F32)],
        compiler_params=_cparams("arbitrary", "arbitrary", "arbitrary"),
        name="attn_prompt",
    )(lam, q, k, v, subln_row)


def _attn_s_kernel(pt_ref, lam_ref, q_ref, kn_ref, vn_ref, sub_ref, *rest, n_pages, n_new, out_scale):
    del pt_ref
    k_refs = rest[:n_pages]
    v_refs = rest[n_pages:2 * n_pages]
    o_ref = rest[2 * n_pages]
    r = n_new * N_HEADS
    qs = _split_components(q_ref[...])
    row = lax.broadcasted_iota(jnp.int32, (2 * r, 1), 0)
    row_h = row % N_HEADS
    row_t = (row // N_HEADS) % n_new
    kcols = PAGE_SIZE * N_HEADS
    col_h = lax.broadcasted_iota(jnp.int32, (1, kcols), 1) % N_HEADS
    same_head = row_h == col_h

    m = jnp.full((2 * r, VAL_DIM), -jnp.inf, F32)
    acc = jnp.zeros((2 * r, 2 * VAL_DIM), F32)
    for p_i in range(n_pages):
        k = k_refs[p_i][...].reshape(kcols, VAL_DIM).astype(BF16)
        v = v_refs[p_i][...].reshape(kcols, VAL_DIM).astype(BF16)
        s = jnp.where(same_head, _dot_nt(qs, k), -jnp.inf)
        m, acc = _online_update(s, m, acc, _with_ones(v))

    ncol = lax.broadcasted_iota(jnp.int32, (1, VAL_DIM), 1)
    new_ok = (ncol < r) & (row_h == ncol % N_HEADS) & (ncol // N_HEADS <= row_t)
    pad = jnp.zeros((VAL_DIM - r, VAL_DIM), BF16)
    kn = jnp.concatenate([kn_ref[...], pad], axis=0)
    vn = jnp.concatenate([vn_ref[...], pad], axis=0)
    s = jnp.where(new_ok, _dot_nt(qs, kn), -jnp.inf)
    m, acc = _online_update(s, m, acc, _with_ones(vn))
    o_ref[...] = _diff_finish(acc, lam_ref[0], sub_ref[...], out_scale, r).astype(BF16)


def _attn_sample(q3, kn3, vn3, cache_k, cache_v, page_table, lam, subln_row, j, out_scale):
    n_seq, r, _ = q3.shape
    n_pages = page_table.shape[1]
    n_new = r // N_HEADS
    row3 = pl.BlockSpec((None, r, VAL_DIM), lambda b, pt: (b, 0, 0))

    def page_spec(p):
        return pl.BlockSpec((None, None, PAGE_SIZE, N_HEADS, VAL_DIM), lambda b, pt: (j, pt[b, p], 0, 0, 0))

    grid_spec = pltpu.PrefetchScalarGridSpec(
        num_scalar_prefetch=1,
        grid=(n_seq,),
        in_specs=[pl.BlockSpec(memory_space=pltpu.SMEM), row3, row3, row3,
                  pl.BlockSpec((1, VAL_DIM), lambda b, pt: (0, 0))]
        + [page_spec(p) for p in range(n_pages)] * 2,
        out_specs=row3,
    )
    return pl.pallas_call(
        functools.partial(_attn_s_kernel, n_pages=n_pages, n_new=n_new, out_scale=out_scale),
        grid_spec=grid_spec,
        out_shape=jax.ShapeDtypeStruct((n_seq, r, VAL_DIM), BF16),
        compiler_params=_cparams("arbitrary"),
        name="attn_sample",
    )(page_table, lam, q3, kn3, vn3, subln_row, *([cache_k] * n_pages), *([cache_v] * n_pages))


def _wo_kernel(x_ref, o_ref, gt_ref, w_ref, xo_ref):
    xo_ref[...] = x_ref[...] + _mod(gt_ref, x_ref.shape[0]) * _dot(o_ref[...], w_ref[...])


def _attn_out(x, o, grp, layer, wo_bf, tm):
    t_tot, d = x.shape
    tok = pl.BlockSpec((tm, d), lambda t: (t, 0))
    return pl.pallas_call(
        _wo_kernel,
        grid=(t_tot // tm,),
        in_specs=[tok, tok, grp.mod_spec(layer, 2, tm), pl.BlockSpec((d, d), lambda t: (0, 0))],
        out_specs=tok,
        out_shape=jax.ShapeDtypeStruct((t_tot, d), F32),
        compiler_params=_cparams("arbitrary"),
        name="attn_out",
    )(x, o, grp.mods, wo_bf)


def _route(logits_t, bias):
    s_all = jax.nn.sigmoid(logits_t)
    sb_all = s_all + bias
    s = [s_all[e:e + 1, :] for e in range(N_EXPERTS)]
    sb = [sb_all[e:e + 1, :] for e in range(N_EXPERTS)]
    n = EXPERTS_PER_GROUP
    gscore = []
    for g in range(N_EXPERT_GROUPS):
        v = sb[g * n:(g + 1) * n]
        best = v[0] + v[1]
        for a in range(n):
            for b in range(a + 1, n):
                if (a, b) != (0, 1):
                    best = jnp.maximum(best, v[a] + v[b])
        gscore.append(best)
    gsel = jnp.zeros_like(gscore[0], dtype=jnp.int32)
    gbest = gscore[0]
    for g in range(1, N_EXPERT_GROUPS):
        better = gscore[g] > gbest
        gsel = jnp.where(better, g, gsel)
        gbest = jnp.where(better, gscore[g], gbest)

    def pick(rows, r):
        out = rows[r]
        for g in range(1, N_EXPERT_GROUPS):
            out = jnp.where(gsel == g, rows[g * n + r], out)
        return out

    vb = [pick(sb, r) for r in range(n)]
    vu = [pick(s, r) for r in range(n)]

    def argmax_first(vals):
        idx = jnp.zeros_like(gsel)
        best = vals[0]
        for r in range(1, n):
            better = vals[r] > best
            idx = jnp.where(better, r, idx)
            best = jnp.where(better, vals[r], best)
        return idx

    i1 = argmax_first(vb)
    i2 = argmax_first([jnp.where(i1 == r, -jnp.inf, vb[r]) for r in range(n)])

    def take(vals, idx):
        out = vals[0]
        for r in range(1, n):
            out = jnp.where(idx == r, vals[r], out)
        return out

    w1 = take(vu, i1)
    w2 = take(vu, i2)
    tot = w1 + w2
    e1 = gsel * n + i1
    e2 = gsel * n + i2
    erow = lax.broadcasted_iota(jnp.int32, logits_t.shape, 0)
    return jnp.where(erow == e1, w1 / tot, 0.0) + jnp.where(erow == e2, w2 / tot, 0.0)


def _router_logits(h2, h2_bf, rw):
    rw_hi = rw.astype(BF16)
    rw_lo = (rw - rw_hi.astype(F32)).astype(BF16)
    h_lo = (h2 - h2_bf.astype(F32)).astype(BF16)
    return _dot_nt(rw_hi, h2_bf) + (_dot_nt(rw_hi, h_lo) + _dot_nt(rw_lo, h2_bf))


def _moe_dense_kernel(x_ref, sh_ref, sc_ref, gt_ref, nrm_ref, rw_ref, rb_ref, wg_ref, wu_ref, wd_ref, *rest,
                      tm, n_e, final):
    fg_ref = rest[0] if final else None
    xo_ref, h_ref, gates_ref, acc_ref = rest[1:] if final else rest
    e = pl.program_id(1)

    @pl.when(e == 0)
    def _():
        h2 = _normmod(x_ref[...], nrm_ref[...], _mod(sc_ref, tm), _mod(sh_ref, tm))
        h2_bf = h2.astype(BF16)
        h_ref[...] = h2_bf
        gates_t = _route(_router_logits(h2, h2_bf, rw_ref[...]), rb_ref[...])
        pad = jnp.zeros((128 - N_EXPERTS, tm), F32)
        gates_ref[...] = jnp.concatenate([gates_t, pad], axis=0).T
        acc_ref[...] = jnp.zeros(acc_ref.shape, F32)

    h = h_ref[...]
    gates = gates_ref[...]
    lane = lax.broadcasted_iota(jnp.int32, (1, 128), 1)
    acts = []
    for u in range(n_e):
        hg = _dot(h, wg_ref[u].astype(BF16))
        hu = _dot(h, wu_ref[u].astype(BF16))
        gate = jnp.sum(jnp.where(lane == e * n_e + u, gates, 0.0), axis=-1, keepdims=True)
        acts.append(((hg * jax.nn.sigmoid(hg)) * hu * gate).astype(BF16))
    act = jnp.concatenate(acts, axis=1)
    acc_ref[...] += _dot(act, wd_ref[...].reshape(n_e * D_EXPERT, D_MODEL).astype(BF16))

    @pl.when(e == N_EXPERTS // n_e - 1)
    def _():
        xo = x_ref[...] + _mod(gt_ref, tm) * acc_ref[...]
        if final:
            xo = (xo * lax.rsqrt(jnp.mean(xo * xo, axis=-1, keepdims=True) + EPS)) * fg_ref[...]
        xo_ref[...] = xo


def _moe_dense(x, grp, layer, norm_ffn, router_wt, router_b, wg, wu, wd, tm, final_gain=None, n_e=2):
    t_tot, d = x.shape
    tok = pl.BlockSpec((tm, d), lambda t, e: (t, 0))
    final = final_gain is not None
    extra_specs, extra_args = ([pl.BlockSpec((1, d), lambda t, e: (0, 0))], [final_gain]) if final else ([], [])
    return pl.pallas_call(
        functools.partial(_moe_dense_kernel, tm=tm, n_e=n_e, final=final),
        grid=(t_tot // tm, N_EXPERTS // n_e),
        in_specs=[tok, grp.mod_spec(layer, 3, tm), grp.mod_spec(layer, 4, tm), grp.mod_spec(layer, 5, tm),
                  _row_spec(layer),
                  pl.BlockSpec((N_EXPERTS, d), lambda t, e: (0, 0)),
                  pl.BlockSpec((N_EXPERTS, 1), lambda t, e: (0, 0)),
                  pl.BlockSpec((None, n_e, d, D_EXPERT), lambda t, e: (layer, e, 0, 0)),
                  pl.BlockSpec((None, n_e, d, D_EXPERT), lambda t, e: (layer, e, 0, 0)),
                  pl.BlockSpec((None, n_e, D_EXPERT, d), lambda t, e: (layer, e, 0, 0))] + extra_specs,
        out_specs=tok,
        out_shape=jax.ShapeDtypeStruct((t_tot, d), F32),
        scratch_shapes=[pltpu.VMEM((tm, d), BF16), pltpu.VMEM((tm, 128), F32),
                        pltpu.VMEM((tm, d), F32)],
        compiler_params=_cparams("arbitrary", "arbitrary"),
        name="moe_dense",
    )(x, grp.mods, grp.mods, grp.mods, norm_ffn, router_wt, router_b, wg, wu, wd, *extra_args)


def kernel(x_prompt, x_sample, c_prompt, c_sample, state_pool, cache_k, cache_v, page_table, ada_w, ada_b, norm_mix, norm_ffn, norm_final, pool_w, pool_scale, attn_w_qkv, attn_w_o, lambda_q1, lambda_k1, lambda_q2, lambda_k2, subln, router_w, router_bias, moe_w_gate, moe_w_up, moe_w_down):
    bp, n_p, d = x_prompt.shape
    bs, n_s, _ = x_sample.shape
    depth = ada_w.shape[0]
    past_len = page_table.shape[1] * PAGE_SIZE
    tm_p, tm_s = 512, 256

    mods_s, mods_p = _ada_mods(c_sample, c_prompt, ada_w, ada_b)
    grp_p = _Group(mods_p.reshape(depth, bp, 1, N_MODS * d), many_per_tile=False, seq=n_p)
    grp_s = _Group(mods_s, many_per_tile=True, seq=n_s)

    xp = x_prompt.reshape(bp * n_p, d)
    xs = x_sample.reshape(bs * n_s, d)

    norm_mix3 = norm_mix.reshape(depth, 1, d)
    norm_ffn3 = norm_ffn.reshape(depth, 1, d)
    pool_w_bf = pool_w.astype(BF16)
    pool_scale3 = pool_scale.reshape(-1, 1, d)
    wqkv_bf = attn_w_qkv.astype(BF16)
    wo_bf = attn_w_o.astype(BF16)
    router_wt = router_w.T
    router_b = router_bias.reshape(N_EXPERTS, 1)

    cos_p, sin_p = _rope_tables(jnp.arange(n_p))
    cos_s, sin_s = _rope_tables(past_len + jnp.arange(n_s))
    cos_s = jnp.tile(cos_s, (tm_s // n_s, 1))
    sin_s = jnp.tile(sin_s, (tm_s // n_s, 1))

    pool_p, pool_s = [], []
    rows_p = rows_s = None
    g_row = norm_final.reshape(1, d)
    for i in range(depth):
        j = i // 2
        if i % 2 == 0:
            xp, st_p = _pool_prompt(xp, grp_p, i, norm_mix3, pool_w_bf, pool_scale3, j, bp, ts=tm_p)
            state_pad = jnp.pad(state_pool[j], ((0, 0), (1, 0), (0, 0)))
            xs, st_s = _pool_sample(xs, state_pad, grp_s, i, norm_mix3, pool_w_bf, pool_scale3, j, n_s)
            pool_p.append(st_p[:, 1:])
            pool_s.append(st_s[:, 1:])
        else:
            lam_init = 0.8 - 0.6 * math.exp(-0.3 * i)
            lam = (jnp.exp(jnp.sum(lambda_q1[j] * lambda_k1[j])) - jnp.exp(jnp.sum(lambda_q2[j] * lambda_k2[j]))
                   + lam_init).reshape(1).astype(F32)
            sub_row = subln[j].reshape(1, VAL_DIM)
            out_scale = 1.0 - lam_init

            q, kb, vb, *rows_p = _qkv(xp, grp_p, i, norm_mix3, wqkv_bf[j], cos_p, sin_p, tm_p, rows_p)
            o = _attn_prompt(q, kb, vb, lam, sub_row, bp, n_p, out_scale)
            xp = _attn_out(xp, o, grp_p, i, wo_bf[j], tm_p)

            q, kb, vb, *rows_s = _qkv(xs, grp_s, i, norm_mix3, wqkv_bf[j], cos_s, sin_s, tm_s, rows_s)
            r = n_s * N_HEADS
            o3 = _attn_sample(q.reshape(bs, r, VAL_DIM), kb.reshape(bs, r, VAL_DIM), vb.reshape(bs, r, VAL_DIM),
                              cache_k, cache_v, page_table, lam, sub_row, j, out_scale)
            xs = _attn_out(xs, o3.reshape(bs * n_s, d), grp_s, i, wo_bf[j], tm_s)

        final_gain = g_row if i == depth - 1 else None
        xp = _moe_dense(xp, grp_p, i, norm_ffn3, router_wt, router_b, moe_w_gate, moe_w_up, moe_w_down,
                        2 * tm_p, final_gain)
        xs = _moe_dense(xs, grp_s, i, norm_ffn3, router_wt, router_b, moe_w_gate, moe_w_up, moe_w_down,
                        min(2 * tm_p, bs * n_s), final_gain)

    n_attn = depth // 2
    return (xp.reshape(bp, n_p, d), xs.reshape(bs, n_s, d), jnp.stack(pool_p), jnp.stack(pool_s),
            rows_p[0].reshape(n_attn, bp, n_p, N_HEADS, VAL_DIM), rows_p[1].reshape(n_attn, bp, n_p, N_HEADS, VAL_DIM),
            rows_s[0].reshape(n_attn, bs, n_s, N_HEADS, VAL_DIM), rows_s[1].reshape(n_attn, bs, n_s, N_HEADS, VAL_DIM))
```
